```python
import jax, jax.numpy as jnp
from jax import lax
import numpy as np

D_MODEL = 1024
BATCH = 1
SEQ = 16384
DEPTH = 1

CHUNK = 64
HEAD_DIM = 64
RWKV_HEADS = 8
D_RWKV = RWKV_HEADS * HEAD_DIM
DECAY_LORA = 64
ICLR_LORA = 64
DECAY_SCALE = 0.606531
GN_EPS = 64e-5
ATT_HEADS = 8
D_ATT = ATT_HEADS * HEAD_DIM
IDX_HEADS = 8
IDX_DIM = 64
TOPK_MAX = 256
Q_BLOCK = 128
NORM_EPS = 1e-6

N_RWKV_COLS = 3 * D_RWKV + DECAY_LORA + ICLR_LORA + D_RWKV
N_DSA_COLS = 3 * D_ATT + D_ATT + IDX_HEADS * IDX_DIM + IDX_DIM + IDX_HEADS
N_MERGE_COLS = 2 * D_MODEL
N_IN = N_RWKV_COLS + N_DSA_COLS + N_MERGE_COLS
RWKV_SPLITS = (D_RWKV, 2 * D_RWKV, 3 * D_RWKV, 3 * D_RWKV + DECAY_LORA,
               3 * D_RWKV + DECAY_LORA + ICLR_LORA)
DSA_SPLITS = (D_ATT, 2 * D_ATT, 3 * D_ATT, 4 * D_ATT,
              4 * D_ATT + IDX_HEADS * IDX_DIM,
              4 * D_ATT + IDX_HEADS * IDX_DIM + IDX_DIM)

kernel_name = "hybrid_rwkv7_dsa_gated_block"


def rms_norm(x, g, eps=NORM_EPS):
    xf = x.astype(jnp.float32)
    y = xf * lax.rsqrt(jnp.mean(xf * xf, axis=-1, keepdims=True) + eps)
    return (y * g.astype(jnp.float32)).astype(x.dtype)


def rwkv7_time_mix(pa, mu, w0, w_up, a0, a_up, k_k, k_a, r_k, gn_w, gn_b):
    B, S, _ = pa.shape
    prev = jnp.pad(pa, ((0, 0), (1, 0), (0, 0)))[:, :S]
    pa = pa + mu * (prev - pa)
    r, k, v, wd, ad, g = jnp.split(pa, RWKV_SPLITS, axis=-1)
    w = jnp.exp(-DECAY_SCALE * jax.nn.sigmoid((w0 + jnp.tanh(wd) @ w_up).astype(jnp.float32)))
    a = jax.nn.sigmoid((a0 + ad @ a_up).astype(jnp.float32))
    kk = (k * k_k).astype(jnp.float32).reshape(B, S, RWKV_HEADS, HEAD_DIM)
    kk = kk / jnp.maximum(jnp.linalg.norm(kk, axis=-1, keepdims=True), 1e-12)
    k = (k * (1.0 + (a - 1.0) * k_a)).astype(jnp.float32)
    hs = lambda t: t.astype(jnp.float32).reshape(B, S, RWKV_HEADS, HEAD_DIM)
    r_h, w_h, k_h, v_h, a_h = hs(r), hs(w), hs(k), hs(v), hs(a)
    b_h = kk * a_h
    tm = lambda t: jnp.moveaxis(t, 1, 0)

    def step(state, inp):
        r_t, w_t, k_t, v_t, nkk_t, b_t = inp
        sa = jnp.einsum('bhvk,bhk->bhv', state, nkk_t)
        state = (state * w_t[:, :, None, :]
                 + sa[..., None] * b_t[:, :, None, :]
                 + v_t[..., None] * k_t[:, :, None, :])
        y_t = jnp.einsum('bhvk,bhk->bhv', state, r_t)
        return state, y_t

    s0 = jnp.zeros((B, RWKV_HEADS, HEAD_DIM, HEAD_DIM), jnp.float32)
    _, y = lax.scan(step, s0, (tm(r_h), tm(w_h), tm(k_h), tm(v_h), tm(-kk), tm(b_h)))
    y = jnp.moveaxis(y, 0, 1)
    mean = jnp.mean(y, axis=-1, keepdims=True)
    var = jnp.mean(jnp.square(y - mean), axis=-1, keepdims=True)
    y = (y - mean) * lax.rsqrt(var + GN_EPS)
    y = y * gn_w.reshape(RWKV_HEADS, HEAD_DIM) + gn_b.reshape(RWKV_HEADS, HEAD_DIM)
    bonus = jnp.sum(r_h * k_h * r_k.astype(jnp.float32), axis=-1, keepdims=True) * v_h
    y = (y + bonus).reshape(B, S, D_RWKV).astype(pa.dtype)
    return y, g


def dsa_sparse_attention(q, k, v, iq, ik, iw):
    B, S, H, Dh = q.shape
    topk = min(TOPK_MAX, S // 4)
    nb = S // Q_BLOCK
    key_chunk = jnp.arange(S) // CHUNK
    ik32 = ik.astype(jnp.float32)
    iw32 = iw.astype(jnp.float32) * (IDX_HEADS ** -0.5)
    blocks = lambda t: jnp.moveaxis(t.reshape((B, nb, Q_BLOCK) + t.shape[2:]), 1, 0)
    q_pos = jnp.arange(S).reshape(nb, Q_BLOCK)
    scale = HEAD_DIM ** -0.5

    def block_fn(args):
        qb, iqb, iwb, pos = args
        q_chunk = pos // CHUNK
        sc = jnp.einsum('bqhd,bsd->bqhs', iqb.astype(jnp.float32), ik32) * (IDX_DIM ** -0.5)
        idx_score = jnp.einsum('bqhs,bqh->bqs', jax.nn.relu(sc), iwb)
        adm = key_chunk[None, :] <= q_chunk[:, None]
        idx_score = jnp.where(adm[None], idx_score, -jnp.inf)
        _, sel = lax.top_k(idx_score, topk)
        valid = (sel // CHUNK) <= q_chunk[None, :, None]
        k_sel = jax.vmap(lambda kb, ib: kb[ib])(k, sel)
        v_sel = jax.vmap(lambda vb, ib: vb[ib])(v, sel)
        logits = jnp.einsum('bqhd,bqkhd->bqhk', qb.astype(jnp.float32),
                            k_sel.astype(jnp.float32)) * scale
        logits = jnp.where(valid[:, :, None, :], logits, -jnp.inf)
        p = jax.nn.softmax(logits, axis=-1)
        return jnp.einsum('bqhk,bqkhd->bqhd', p, v_sel.astype(jnp.float32)).astype(q.dtype)

    out = lax.map(block_fn, (blocks(q), blocks(iq), blocks(iw32), q_pos))
    return jnp.moveaxis(out, 0, 1).reshape(B, S, H, Dh)


def hybrid_layer(x, c, norm_w, w_ada, b_ada, w_in, mu, w0, w_up, a0, a_up, k_k, k_a,
                 r_k, gn_w, gn_b, q_gain, k_gain, w_a_out, w_b_out, w_o):
    B, S, D = x.shape
    mod = jax.nn.silu(c) @ w_ada + b_ada
    shift, scale, gate = jnp.split(mod, 3, axis=-1)
    h = rms_norm(x, norm_w) * (1.0 + scale[:, None, :]) + shift[:, None, :]
    p = h @ w_in
    pa = p[..., :N_RWKV_COLS]
    pb = p[..., N_RWKV_COLS:N_RWKV_COLS + N_DSA_COLS]
    pg = p[..., N_RWKV_COLS + N_DSA_COLS:]
    ya, ga = rwkv7_time_mix(pa, mu, w0, w_up, a0, a_up, k_k, k_a, r_k, gn_w, gn_b)
    ya = (ya * jax.nn.silu(ga)) @ w_a_out
    q, k, v, gb, iq, ik, iw = jnp.split(pb, DSA_SPLITS, axis=-1)
    heads = lambda t: t.reshape(B, S, ATT_HEADS, HEAD_DIM)
    q = rms_norm(heads(q), q_gain)
    k = rms_norm(heads(k), k_gain)
    v = heads(v)
    iq = iq.reshape(B, S, IDX_HEADS, IDX_DIM)
    yb = dsa_sparse_attention(q, k, v, iq, ik, iw).reshape(B, S, D_ATT)
    yb = (yb * jax.nn.silu(gb)) @ w_b_out
    gm_a, gm_b = jnp.split(pg, 2, axis=-1)
    merged = jax.nn.sigmoid(gm_a) * ya + jax.nn.sigmoid(gm_b) * yb
    out = merged @ w_o
    return x + gate[:, None, :] * out


def setup_inputs(seed: int = 0) -> dict:
    key = jax.random.key(seed)
    ks = jax.random.split(key, 24)
    nrm = lambda k, shape, s: jax.random.normal(k, shape, jnp.float32) * s
    L = DEPTH
    return {
        "x": nrm(ks[0], (BATCH, SEQ, D_MODEL), 1.0),
        "c": nrm(ks[1], (BATCH, D_MODEL), 1.0),
        "norm_w": 1.0 + nrm(ks[2], (L, D_MODEL), 0.02),
        "w_ada": nrm(ks[3], (L, D_MODEL, 3 * D_MODEL), 0.5 * D_MODEL ** -0.5),
        "b_ada": nrm(ks[4], (L, 3 * D_MODEL), 0.02),
        "w_in": nrm(ks[5], (L, D_MODEL, N_IN), D_MODEL ** -0.5),
        "mu": jax.random.uniform(ks[6], (L, N_RWKV_COLS), jnp.float32),
        "w0": nrm(ks[7], (L, D_RWKV), 0.5),
        "w_up": nrm(ks[8], (L, DECAY_LORA, D_RWKV), 0.5 * DECAY_LORA ** -0.5),
        "a0": nrm(ks[9], (L, D_RWKV), 0.1),
        "a_up": nrm(ks[10], (L, ICLR_LORA, D_RWKV), 0.5 * ICLR_LORA ** -0.5),
        "k_k": 0.85 + nrm(ks[11], (L, D_RWKV), 0.05),
        "k_a": 1.0 + nrm(ks[12], (L, D_RWKV), 0.05),
        "r_k": nrm(ks[13], (L, RWKV_HEADS, HEAD_DIM), 0.1),
        "gn_w": 1.0 + nrm(ks[14], (L, D_RWKV), 0.02),
        "gn_b": nrm(ks[15], (L, D_RWKV), 0.02),
        "q_gain": 1.0 + nrm(ks[16], (L, HEAD_DIM), 0.02),
        "k_gain": 1.0 + nrm(ks[17], (L, HEAD_DIM), 0.02),
        "w_a_out": nrm(ks[18], (L, D_RWKV, D_MODEL), D_RWKV ** -0.5),
        "w_b_out": nrm(ks[19], (L, D_ATT, D_MODEL), D_ATT ** -0.5),
        "w_o": nrm(ks[20], (L, D_MODEL, D_MODEL), D_MODEL ** -0.5),
    }


def reference(x, c, norm_w, w_ada, b_ada, w_in, mu, w0, w_up, a0, a_up, k_k, k_a,
              r_k, gn_w, gn_b, q_gain, k_gain, w_a_out, w_b_out, w_o):
    for l in range(DEPTH):
        x = hybrid_layer(x, c, norm_w[l], w_ada[l], b_ada[l], w_in[l], mu[l], w0[l],
                         w_up[l], a0[l], a_up[l], k_k[l], k_k[l] * 0.0 + k_a[l] if False else k_a[l],
                         r_k[l], gn_w[l], gn_b[l], q_gain[l], k_gain[l],
                         w_a_out[l], w_b_out[l], w_o[l])
    return x
```

```python
import functools

import numpy as np
import jax
import jax.numpy as jnp
from jax import lax
from jax.experimental import pallas as pl
from jax.experimental.pallas import tpu as pltpu

F32 = jnp.float32
BF16 = jnp.bfloat16
I32 = jnp.int32
HIGHEST = lax.Precision.HIGHEST

CHUNK = 64
HEAD_DIM = 64
N_HEADS = 8
D_BRANCH = N_HEADS * HEAD_DIM
LORA = 64
DECAY_SCALE = 0.606531
GN_EPS = 64e-5
NORM_EPS = 1e-6
TOPK_MAX = 256
Q_BLOCK = 128

LANES = 128
N_PAIRS = D_BRANCH // LANES
VMEM_LIMIT = 56 * 1024 * 1024

KEY_BLOCK = 1024
KEY_SUB = 256
COUNT_BLOCK = 512
INT_MIN = -(2 ** 31)
NEG_BIG = -1e30

NA = 4 * D_BRANCH + 2 * LANES
NI = N_HEADS * 4 * HEAD_DIM + 4 * HEAD_DIM + LANES


def _sigmoid(x):
    return 1.0 / (1.0 + jnp.exp(-x))


def _hdot(a, b):
    return jnp.dot(a, b, precision=HIGHEST, preferred_element_type=F32)


def _hdot_nt(a, b):
    return lax.dot_general(a, b, (((1,), (1,)), ((), ())), precision=HIGHEST,
                           preferred_element_type=F32)


def _hdot_tn(a, b):
    return lax.dot_general(a, b, (((0,), (0,)), ((), ())), precision=HIGHEST,
                           preferred_element_type=F32)


def _dot_nt(a, b):
    return lax.dot_general(a, b, (((1,), (1,)), ((), ())), preferred_element_type=F32)


def _iota(shape, dim):
    return lax.broadcasted_iota(I32, shape, dim)


def _ada_kernel(c_ref, w_ref, b_ref, o_ref):
    c = c_ref[...]
    s = c * _sigmoid(c)
    o_ref[...] = _hdot(s, w_ref[...]) + b_ref[...]


def _ada(c8, w_ada, b_ada):
    d = c8.shape[1]
    n = w_ada.shape[1]
    return pl.pallas_call(
        _ada_kernel,
        grid=(n // d,),
        in_specs=[pl.BlockSpec((8, d), lambda j: (0, 0)),
                  pl.BlockSpec((d, d), lambda j: (0, j)),
                  pl.BlockSpec((1, d), lambda j: (0, j))],
        out_specs=pl.BlockSpec((8, d), lambda j: (0, j)),
        out_shape=jax.ShapeDtypeStruct((8, n), F32),
        name="ada",
    )(c8, w_ada, b_ada)


def _inproj_kernel(x_ref, nw_ref, sc_ref, sh_ref, wa_ref, wq_ref, wi_ref, wg_ref,
                   pa_ref, pq_ref, pi_ref, pg_ref):
    x = x_ref[...]
    ms = jnp.mean(x * x, axis=-1, keepdims=True)
    h = x * lax.rsqrt(ms + NORM_EPS) * nw_ref[...]
    h = h * (1.0 + sc_ref[...]) + sh_ref[...]
    hb = h.astype(BF16)
    pa_ref[...] = jnp.dot(hb, wa_ref[...], preferred_element_type=F32)
    pq_ref[...] = jnp.dot(hb, wq_ref[...], preferred_element_type=F32)
    pi_ref[...] = jnp.dot(hb, wi_ref[...], preferred_element_type=F32)
    pg_ref[...] = jnp.dot(hb, wg_ref[...], preferred_element_type=F32)


def _inproj(x2, nw, scale, shift, wa, wq, wi, wg, tm):
    s, d = x2.shape
    row = lambda n: pl.BlockSpec((tm, n), lambda i: (i, 0))
    vec = pl.BlockSpec((1, d), lambda i: (0, 0))
    wsp = lambda w: pl.BlockSpec(w.shape, lambda i: (0, 0))
    return pl.pallas_call(
        _inproj_kernel,
        grid=(s // tm,),
        in_specs=[row(d), vec, vec, vec, wsp(wa), wsp(wq), wsp(wi), wsp(wg)],
        out_specs=[row(wa.shape[1]), row(wq.shape[1]), row(wi.shape[1]), row(wg.shape[1])],
        out_shape=[jax.ShapeDtypeStruct((s, w.shape[1]), F32) for w in (wa, wq, wi, wg)],
        compiler_params=pltpu.CompilerParams(dimension_semantics=("arbitrary",),
                                             vmem_limit_bytes=VMEM_LIMIT),
        name="inproj",
    )(x2, nw, scale, shift, wa, wq, wi, wg)


def _rwkv_kernel(pa_ref, mu_ref, w0_ref, a0_ref, kk_ref, ka_ref, rk_ref, gnw_ref, gnb_ref,
                 wup_ref, aup_ref, o_ref, s_ref, carry_ref, *, n_chunks):
    @pl.when(pl.program_id(0) == 0)
    def _():
        s_ref[...] = jnp.zeros_like(s_ref)
        carry_ref[...] = jnp.zeros_like(carry_ref)

    row = _iota((LANES, LANES), 0)
    col = _iota((LANES, LANES), 1)
    strict = row > col
    incl = row >= col
    same_head = (row < HEAD_DIM) == (col < HEAD_DIM)
    bd_ones = jnp.where(same_head, 1.0, 0.0).astype(F32)
    bd_avg = bd_ones * (1.0 / HEAD_DIM)
    eye = jnp.where(row == col, 1.0, 0.0).astype(F32)
    tri = jnp.where(_iota((CHUNK, CHUNK), 0) >= _iota((CHUNK, CHUNK), 1), 1.0, 0.0).astype(F32)
    lane_h0 = _iota((CHUNK, LANES), 1) < HEAD_DIM
    first_row = _iota((CHUNK, NA), 0) == 0

    def stack(a):
        return jnp.concatenate([jnp.where(lane_h0, a, 0.0), jnp.where(lane_h0, 0.0, a)], axis=0)

    def chunk_body(c, carry):
        r0 = pl.multiple_of(c * CHUNK, CHUNK)
        x = pa_ref[pl.ds(r0, CHUNK), :]
        prev = pltpu.roll(x, 1, 0)
        prev = jnp.where(first_row, carry_ref[7:8, :], prev)
        carry_ref[...] = x[CHUNK - 8:CHUNK, :]
        xs = x + mu_ref[...] * (prev - x)
        d = D_BRANCH
        r = xs[:, 0:d]
        k = xs[:, d:2 * d]
        v = xs[:, 2 * d:3 * d]
        g = xs[:, 3 * d:4 * d]
        wd = xs[:, 4 * d:4 * d + LANES]
        ad = xs[:, 4 * d + LANES:4 * d + 2 * LANES]
        lw = -DECAY_SCALE * _sigmoid(w0_ref[...] + _hdot(jnp.tanh(wd), wup_ref[...]))
        a = _sigmoid(a0_ref[...] + _hdot(ad, aup_ref[...]))
        kkx = k * kk_ref[...]
        k2 = k * (1.0 + (a - 1.0) * ka_ref[...])
        bon = r * k2 * rk_ref[...]

        for p in range(N_PAIRS):
            sl = slice(p * LANES, (p + 1) * LANES)
            rp, kp, vp, lwp, ap, kkxp = r[:, sl], k2[:, sl], v[:, sl], lw[:, sl], a[:, sl], kkx[:, sl]
            ss = _hdot(kkxp * kkxp, bd_ones)
            kkp = kkxp / jnp.maximum(jnp.sqrt(ss), 1e-12)
            alpha = -kkp
            beta = kkp * ap
            cl = _hdot(tri, lwp)
            clp = cl - lwp
            cm = cl[CHUNK // 2 - 1:CHUNK // 2, :]
            cend = cl[CHUNK - 1:CHUNK, :]
            e_k = jnp.exp(cm - cl)
            e_h = jnp.exp(cend - cl)
            a_t = alpha * jnp.exp(clp - cm)
            r_t = rp * jnp.exp(cl - cm)
            lhs = jnp.concatenate([stack(a_t), stack(r_t)], axis=0)
            rhs = jnp.concatenate([stack(beta * e_k), stack(kp * e_k)], axis=0)
            gram = _hdot_nt(lhs, rhs)
            n_ab = jnp.where(strict, gram[0:LANES, 0:LANES], 0.0)
            a_ak = jnp.where(strict, gram[0:LANES, LANES:2 * LANES], 0.0)
            m_rb = jnp.where(incl, gram[LANES:2 * LANES, 0:LANES], 0.0)
            m_rk = jnp.where(incl, gram[LANES:2 * LANES, LANES:2 * LANES], 0.0)
            tinv = eye + n_ab
            npow = n_ab
            for _ in range(5):
                npow = _hdot(npow, npow)
                tinv = tinv + _hdot(tinv, npow)
            v_st = stack(vp)
            s0 = s_ref[p]
            u_rhs = _hdot_nt(stack(alpha * jnp.exp(clp)), s0) + _hdot(a_ak, v_st)
            u_st = _hdot(tinv, u_rhs)
            y_st = _hdot_nt(stack(rp * jnp.exp(cl)), s0) + _hdot(m_rb, u_st) + _hdot(m_rk, v_st)
            y = y_st[0:CHUNK] + y_st[CHUNK:2 * CHUNK]
            uv = jnp.concatenate([u_st, v_st], axis=0)
            bk = jnp.concatenate([stack(beta * e_h), stack(kp * e_h)], axis=0)
            s_ref[p] = s0 * jnp.exp(cend) + _hdot_tn(uv, bk)

            mean = _hdot(y, bd_avg)
            dlt = y - mean
            var = _hdot(dlt * dlt, bd_avg)
            yn = dlt * lax.rsqrt(var + GN_EPS) * gnw_ref[:, sl] + gnb_ref[:, sl]
            bonus = _hdot(bon[:, sl], bd_ones) * vp
            gp = g[:, sl]
            o_ref[pl.ds(r0, CHUNK), sl] = (yn + bonus) * (gp * _sigmoid(gp))
        return carry

    lax.fori_loop(0, n_chunks, chunk_body, 0)


def _rwkv(pa, mu, w0, a0, k_k, k_a, r_k, gn_w, gn_b, w_up, a_up, tm):
    s = pa.shape[0]
    vec = lambda n: pl.BlockSpec((1, n), lambda i: (0, 0))
    lora = pl.BlockSpec((LANES, D_BRANCH), lambda i: (0, 0))
    return pl.pallas_call(
        functools.partial(_rwkv_kernel, n_chunks=tm // CHUNK),
        grid=(s // tm,),
        in_specs=[pl.BlockSpec((tm, NA), lambda i: (i, 0)), vec(NA)] + [vec(D_BRANCH)] * 7
                 + [lora, lora],
        out_specs=pl.BlockSpec((tm, D_BRANCH), lambda i: (i, 0)),
        out_shape=jax.ShapeDtypeStruct((s, D_BRANCH), F32),
        scratch_shapes=[pltpu.VMEM((N_PAIRS, LANES, LANES), F32),
                        pltpu.VMEM((8, NA), F32)],
        compiler_params=pltpu.CompilerParams(dimension_semantics=("arbitrary",),
                                             vmem_limit_bytes=VMEM_LIMIT),
        name="rwkv",
    )(pa, mu, w0, a0, k_k, k_a, r_k, gn_w, gn_b, w_up, a_up)


def _split_hi_lo(x, take_hi):
    hi = x.astype(BF16)
    lo = (x - hi.astype(F32)).astype(BF16)
    return jnp.where(take_hi, hi, lo)


def _dprep_kernel(pq_ref, pi_ref, qg_ref, kg_ref, qs_ref, ka_ref, kb_ref, v_ref,
                  iq_ref, ik_ref, iw_ref):
    tm = pq_ref.shape[0]
    d = D_BRANCH
    row = _iota((LANES, LANES), 0)
    col = _iota((LANES, LANES), 1)
    bd_avg = jnp.where((row < HEAD_DIM) == (col < HEAD_DIM), 1.0 / HEAD_DIM, 0.0).astype(F32)
    lane_h0 = _iota((tm, LANES), 1) < HEAD_DIM
    for p in range(N_PAIRS):
        sl = slice(p * LANES, (p + 1) * LANES)
        q = pq_ref[:, p * LANES:(p + 1) * LANES]
        k = pq_ref[:, d + p * LANES:d + (p + 1) * LANES]
        qn = q * lax.rsqrt(_hdot(q * q, bd_avg) + NORM_EPS) * qg_ref[:, sl]
        kn = k * lax.rsqrt(_hdot(k * k, bd_avg) + NORM_EPS) * kg_ref[:, sl]
        qs_ref[:, sl] = (qn * (HEAD_DIM ** -0.5)).astype(BF16)
        knb = kn.astype(BF16)
        ka_ref[:, sl] = jnp.where(lane_h0, knb, jnp.zeros_like(knb))
        kb_ref[:, sl] = jnp.where(lane_h0, jnp.zeros_like(knb), knb)
    v_ref[...] = pq_ref[:, 2 * d:3 * d].astype(BF16)
    lane4 = _iota((tm, 4 * HEAD_DIM), 1) // HEAD_DIM
    q_hi = (lane4 % 2) == 0
    k_hi = lane4 < 2
    for h in range(N_HEADS):
        sl = slice(h * 4 * HEAD_DIM, (h + 1) * 4 * HEAD_DIM)
        iq_ref[:, sl] = _split_hi_lo(pi_ref[:, sl], q_hi)
    off = N_HEADS * 4 * HEAD_DIM
    ik_ref[...] = _split_hi_lo(pi_ref[:, off:off + 4 * HEAD_DIM], k_hi)
    iw_ref[...] = pi_ref[:, off + 4 * HEAD_DIM:off + 4 * HEAD_DIM + LANES] * (
        (N_HEADS ** -0.5) * (HEAD_DIM ** -0.5))


def _dprep(pq, pi, qg, kg, tm):
    s = pq.shape[0]
    row = lambda n: pl.BlockSpec((tm, n), lambda i: (i, 0))
    vec = pl.BlockSpec((1, D_BRANCH), lambda i: (0, 0))
    nq = N_HEADS * 4 * HEAD_DIM
    outs = [(D_BRANCH, BF16)] * 4 + [(nq, BF16), (4 * HEAD_DIM, BF16), (LANES, F32)]
    return pl.pallas_call(
        _dprep_kernel,
        grid=(s // tm,),
        in_specs=[row(pq.shape[1]), row(pi.shape[1]), vec, vec],
        out_specs=[row(n) for n, _ in outs],
        out_shape=[jax.ShapeDtypeStruct((s, n), dt) for n, dt in outs],
        compiler_params=pltpu.CompilerParams(dimension_semantics=("arbitrary",),
                                             vmem_limit_bytes=VMEM_LIMIT),
        name="dprep",
    )(pq, pi, qg, kg)


def _block_tables(s):
    qb_l, kb_l, nkb_l = [], [], []
    for qb in range(s // Q_BLOCK):
        nkb = -(-(Q_BLOCK * (qb + 1)) // KEY_BLOCK)
        for kb in range(nkb):
            qb_l.append(qb)
            kb_l.append(kb)
            nkb_l.append(nkb)
    return (np.asarray(qb_l, np.int32), np.asarray(kb_l, np.int32), np.asarray(nkb_l, np.int32))


def _select_kernel(qb_tab, kb_tab, nkb_tab, iq_ref, iw_ref, ik_ref, mask_ref, keys_ref, *,
                   topk, seq):
    step = pl.program_id(0)
    qb = qb_tab[step]
    kb = kb_tab[step]
    nkb = nkb_tab[step]
    rows = Q_BLOCK

    q_chunk = (qb * Q_BLOCK + _iota((rows, KEY_SUB), 0)) // CHUNK
    lane_pos = _iota((rows, KEY_SUB), 1)
    for kc in range(KEY_BLOCK // KEY_SUB):
        ik = ik_ref[kc * KEY_SUB:(kc + 1) * KEY_SUB, :]
        acc = jnp.zeros((rows, KEY_SUB), F32)
        for h in range(N_HEADS):
            sc = _dot_nt(iq_ref[:, h * 4 * HEAD_DIM:(h + 1) * 4 * HEAD_DIM], ik)
            acc = acc + jnp.maximum(sc, 0.0) * iw_ref[:, h:h + 1]
        bits = pltpu.bitcast(acc, I32)
        key = bits ^ ((bits >> 31) & 0x7FFFFFFF)
        base = kb * KEY_BLOCK + kc * KEY_SUB
        adm = ((base + lane_pos) // CHUNK) <= q_chunk
        key = jnp.where(adm, key, INT_MIN)
        keys_ref[:, pl.ds(pl.multiple_of(base, KEY_SUB), KEY_SUB)] = key

    @pl.when(kb == nkb - 1)
    def _():
        n_cnt = nkb * (KEY_BLOCK // COUNT_BLOCK)

        def count(pred):
            def body(j, part):
                c0 = pl.multiple_of(j * COUNT_BLOCK, COUNT_BLOCK)
                blk = keys_ref[:, pl.ds(c0, COUNT_BLOCK)]
                for i in range(COUNT_BLOCK // LANES):
                    hit = pred(blk[:, i * LANES:(i + 1) * LANES], c0 + i * LANES)
                    part = part + jnp.where(hit, 1, 0)
                return part
            part = lax.fori_loop(0, n_cnt, body, jnp.zeros((rows, LANES), I32))
            return jnp.sum(part, axis=1, keepdims=True)

        def count_ge(cand):
            cb = jnp.broadcast_to(cand, (rows, LANES))
            return count(lambda blk, c0: blk >= cb)

        zero = jnp.zeros((rows, 1), I32)
        t = jnp.where(count_ge(zero) >= topk, zero, jnp.full((rows, 1), INT_MIN, I32))

        def bit_body(i, t):
            cand = t | jnp.left_shift(jnp.int32(1), 30 - i)
            return jnp.where(count_ge(cand) >= topk, cand, t)

        t = lax.fori_loop(0, 31, bit_body, t)
        tau = jnp.maximum(t, INT_MIN + 1)
        tau_b = jnp.broadcast_to(tau, (rows, LANES))
        n_ge = count_ge(tau)
        excess = jnp.max(n_ge) > topk

        def write_mask(sel):
            def body(j, carry):
                c0 = pl.multiple_of(j * COUNT_BLOCK, COUNT_BLOCK)
                blk = keys_ref[:, pl.ds(c0, COUNT_BLOCK)]
                out = jnp.concatenate(
                    [jnp.where(sel(blk[:, i * LANES:(i + 1) * LANES], c0 + i * LANES), 1.0, 0.0)
                     for i in range(COUNT_BLOCK // LANES)], axis=1)
                mask_ref[:, pl.ds(c0, COUNT_BLOCK)] = out.astype(jnp.int8)
                return carry
            lax.fori_loop(0, n_cnt, body, 0)

        @pl.when(jnp.logical_not(excess))
        def _():
            write_mask(lambda blk, c0: blk >= tau_b)

        @pl.when(excess)
        def _():
            n_gt = count(lambda blk, c0: blk > tau_b)
            need = topk - n_gt
            lane = _iota((rows, LANES), 1)

            def count_tie_upto(cut):
                cutb = jnp.broadcast_to(cut, (rows, LANES))
                return count(lambda blk, c0: (blk == tau_b) & ((c0 + lane) <= cutb))

            n_bits = int(seq - 1).bit_length()

            def pos_body(i, cut):
                cand = cut - jnp.left_shift(jnp.int32(1), n_bits - 1 - i)
                ok = count_tie_upto(cand) >= need
                return jnp.where(ok, cand, cut)

            cut0 = jnp.full((rows, 1), (1 << n_bits) - 1, I32)
            cut = lax.fori_loop(0, n_bits, pos_body, cut0)
            cutb = jnp.broadcast_to(cut, (rows, LANES))
            write_mask(lambda blk, c0: (blk > tau_b) | ((blk == tau_b) & ((c0 + lane) <= cutb)))

        def zero_body(j, carry):
            c0 = pl.multiple_of(j * COUNT_BLOCK, COUNT_BLOCK)
            mask_ref[:, pl.ds(c0, COUNT_BLOCK)] = jnp.zeros((rows, COUNT_BLOCK), jnp.int8)
            return carry
        lax.fori_loop(n_cnt, seq // COUNT_BLOCK, zero_body, 0)


def _select(iq, iw, ik, topk):
    s = iq.shape[0]
    qb_t, kb_t, nkb_t = _block_tables(s)
    grid_spec = pltpu.PrefetchScalarGridSpec(
        num_scalar_prefetch=3,
        grid=(len(qb_t),),
        in_specs=[pl.BlockSpec((Q_BLOCK, iq.shape[1]), lambda i, qb, kb, nk: (qb[i], 0)),
                  pl.BlockSpec((Q_BLOCK, LANES), lambda i, qb, kb, nk: (qb[i], 0)),
                  pl.BlockSpec((KEY_BLOCK, ik.shape[1]), lambda i, qb, kb, nk: (kb[i], 0))],
        out_specs=pl.BlockSpec((Q_BLOCK, s), lambda i, qb, kb, nk: (qb[i], 0)),
        scratch_shapes=[pltpu.VMEM((Q_BLOCK, s), I32)],
    )
    return pl.pallas_call(
        functools.partial(_select_kernel, topk=topk, seq=s),
        grid_spec=grid_spec,
        out_shape=jax.ShapeDtypeStruct((s, s), jnp.int8),
        compiler_params=pltpu.CompilerParams(dimension_semantics=("arbitrary",),
                                             vmem_limit_bytes=VMEM_LIMIT),
        name="select",
    )(jnp.asarray(qb_t), jnp.asarray(kb_t), jnp.asarray(nkb_t), iq, iw, ik)


def _attn_kernel(qb_tab, kb_tab, nkb_tab, q_ref, ka_ref, kb_ref, v_ref, mask_ref, o_ref,
                 m_ref, l_ref, acc_ref):
    step = pl.program_id(0)
    kb = kb_tab[step]
    nkb = nkb_tab[step]
    rows = Q_BLOCK

    @pl.when(kb == 0)
    def _():
        m_ref[...] = jnp.full(m_ref.shape, NEG_BIG, F32)
        l_ref[...] = jnp.zeros_like(l_ref)
        acc_ref[...] = jnp.zeros_like(acc_ref)

    lane_h0 = _iota((rows, LANES), 1) < HEAD_DIM
    for kc in range(KEY_BLOCK // KEY_SUB):
        ks = slice(kc * KEY_SUB, (kc + 1) * KEY_SUB)
        sel = mask_ref[:, ks].astype(I32) != 0
        for p in range(N_PAIRS):
            sl = slice(p * LANES, (p + 1) * LANES)
            q = q_ref[:, sl]
            v = v_ref[ks, sl]
            pv = []
            for hh, k_ref in enumerate((ka_ref, kb_ref)):
                h = 2 * p + hh
                s = jnp.where(sel, _dot_nt(q, k_ref[ks, sl]), NEG_BIG)
                m_old = m_ref[h]
                m_new = jnp.maximum(m_old, jnp.max(s, axis=1, keepdims=True))
                scale = jnp.exp(m_old - m_new)
                prob = jnp.where(sel, jnp.exp(s - m_new[:, 0:1]), 0.0)
                l_ref[h] = scale * l_ref[h] + jnp.sum(prob, axis=1, keepdims=True)
                m_ref[h] = m_new
                pv.append((scale, jnp.dot(prob.astype(BF16), v, preferred_element_type=F32)))
            scale2 = jnp.where(lane_h0, pv[0][0], pv[1][0])
            upd = jnp.where(lane_h0, pv[0][1], pv[1][1])
            acc_ref[:, sl] = acc_ref[:, sl] * scale2 + upd

    @pl.when(kb == nkb - 1)
    def _():
        for p in range(N_PAIRS):
            sl = slice(p * LANES, (p + 1) * LANES)
            l2 = jnp.where(lane_h0, l_ref[2 * p], l_ref[2 * p + 1])
            o_ref[:, sl] = acc_ref[:, sl] / l2


def _attn(qs, ka, kb, v, mask):
    s = qs.shape[0]
    qb_t, kb_t, nkb_t = _block_tables(s)
    qspec = pl.BlockSpec((Q_BLOCK, D_BRANCH), lambda i, qb, kb, nk: (qb[i], 0))
    kspec = pl.BlockSpec((KEY_BLOCK, D_BRANCH), lambda i, qb, kb, nk: (kb[i], 0))
    grid_spec = pltpu.PrefetchScalarGridSpec(
        num_scalar_prefetch=3,
        grid=(len(qb_t),),
        in_specs=[qspec, kspec, kspec, kspec,
                  pl.BlockSpec((Q_BLOCK, KEY_BLOCK), lambda i, qb, kb, nk: (qb[i], kb[i]))],
        out_specs=qspec,
        scratch_shapes=[pltpu.VMEM((N_HEADS, Q_BLOCK, LANES), F32),
                        pltpu.VMEM((N_HEADS, Q_BLOCK, LANES), F32),
                        pltpu.VMEM((Q_BLOCK, D_BRANCH), F32)],
    )
    return pl.pallas_call(
        _attn_kernel,
        grid_spec=grid_spec,
        out_shape=jax.ShapeDtypeStruct((s, D_BRANCH), F32),
        compiler_params=pltpu.CompilerParams(dimension_semantics=("arbitrary",),
                                             vmem_limit_bytes=VMEM_LIMIT),
        name="attn",
    )(jnp.asarray(qb_t), jnp.asarray(kb_t), jnp.asarray(nkb_t), qs, ka, kb, v, mask)


def _outproj_kernel(x_ref, za_ref, yb_ref, gb_ref, pg_ref, gate_ref, wa_ref, wb_ref, wo_ref,
                    o_ref):
    d = x_ref.shape[1]
    gb = gb_ref[...]
    zb = yb_ref[...] * (gb * _sigmoid(gb))
    ya = jnp.dot(za_ref[...].astype(BF16), wa_ref[...], preferred_element_type=F32)
    yb = jnp.dot(zb.astype(BF16), wb_ref[...], preferred_element_type=F32)
    merged = _sigmoid(pg_ref[:, 0:d]) * ya + _sigmoid(pg_ref[:, d:2 * d]) * yb
    out = jnp.dot(merged.astype(BF16), wo_ref[...], preferred_element_type=F32)
    o_ref[...] = x_ref[...] + gate_ref[...] * out


def _outproj(x2, za, yb, pq, pg, gate, wa, wb, wo, tm):
    s, d = x2.shape
    row = lambda n: pl.BlockSpec((tm, n), lambda i: (i, 0))
    wsp = lambda w: pl.BlockSpec(w.shape, lambda i: (0, 0))
    gb_col = 3 * D_BRANCH // D_BRANCH
    return pl.pallas_call(
        _outproj_kernel,
        grid=(s // tm,),
        in_specs=[row(d), row(D_BRANCH), row(D_BRANCH),
                  pl.BlockSpec((tm, D_BRANCH), lambda i: (i, gb_col)),
                  row(2 * d), pl.BlockSpec((1, d), lambda i: (0, 0)),
                  wsp(wa), wsp(wb), wsp(wo)],
        out_specs=row(d),
        out_shape=jax.ShapeDtypeStruct((s, d), F32),
        compiler_params=pltpu.CompilerParams(dimension_semantics=("arbitrary",),
                                             vmem_limit_bytes=VMEM_LIMIT),
        name="outproj",
    )(x2, za, yb, pq, pg, gate, wa, wb, wo)


def _pad_cols(w, n):
    return jnp.pad(w, ((0, 0), (0, n - w.shape[1])))


def _layer(x2, c, norm_w, w_ada, b_ada, w_in, mu, w0, w_up, a0, a_up, k_k, k_a, r_k, gn_w,
           gn_b, q_gain, k_gain, w_a_out, w_b_out, w_o):
    s, d = x2.shape
    db = D_BRANCH
    assert s % KEY_BLOCK == 0 and d % LANES == 0
    topk = min(TOPK_MAX, s // 4)

    mod = _ada(jnp.broadcast_to(c, (8, d)), w_ada, b_ada[None, :])[0:1]
    shift, scale, gate = mod[:, 0:d], mod[:, d:2 * d], mod[:, 2 * d:3 * d]

    n_rwkv = 4 * db + 2 * LORA
    def rwkv_cols(w):
        return jnp.concatenate(
            [w[:, 0:3 * db], w[:, 3 * db + 2 * LORA:n_rwkv],
             _pad_cols(w[:, 3 * db:3 * db + LORA], LANES),
             _pad_cols(w[:, 3 * db + LORA:3 * db + 2 * LORA], LANES)], axis=1)
    wa = rwkv_cols(w_in[:, 0:n_rwkv]).astype(BF16)
    mu_a = rwkv_cols(mu[None, :])
    o = n_rwkv
    wq = w_in[:, o:o + 4 * db].astype(BF16)
    o += 4 * db
    w_iq = w_in[:, o:o + N_HEADS * HEAD_DIM].reshape(d, N_HEADS, 1, HEAD_DIM)
    w_iq = jnp.broadcast_to(w_iq, (d, N_HEADS, 4, HEAD_DIM)).reshape(d, N_HEADS * 4 * HEAD_DIM)
    o += N_HEADS * HEAD_DIM
    w_ik = jnp.tile(w_in[:, o:o + HEAD_DIM], (1, 4))
    o += HEAD_DIM
    w_iw = _pad_cols(w_in[:, o:o + N_HEADS], LANES)
    o += N_HEADS
    wi = jnp.concatenate([w_iq, w_ik, w_iw], axis=1).astype(BF16)
    wg = w_in[:, o:o + 2 * d].astype(BF16)

    pa, pq, pi, pg = _inproj(x2, norm_w[None, :], scale, shift, wa, wq, wi, wg, tm=256)

    pad_rows = lambda w: jnp.pad(w, ((0, LANES - w.shape[0]), (0, 0)))
    za = _rwkv(pa, mu_a, w0[None, :], a0[None, :], k_k[None, :], k_a[None, :],
               r_k.reshape(1, db), gn_w[None, :], gn_b[None, :], pad_rows(w_up), pad_rows(a_up),
               tm=512)

    qg = jnp.tile(q_gain, N_HEADS)[None, :]
    kg = jnp.tile(k_gain, N_HEADS)[None, :]
    qs, kna, knb, vb, iq, ik, iw = _dprep(pq, pi, qg, kg, tm=512)
    mask = _select(iq, iw, ik, topk)
    yb = _attn(qs, kna, knb, vb, mask)

    return _outproj(x2, za, yb, pq, pg, gate, w_a_out.astype(BF16), w_b_out.astype(BF16),
                    w_o.astype(BF16), tm=512)


def kernel(x, c, norm_w, w_ada, b_ada, w_in, mu, w0, w_up, a0, a_up, k_k, k_a, r_k, gn_w, gn_b,
           q_gain, k_gain, w_a_out, w_b_out, w_o):
    b, s, d = x.shape
    outs = []
    for bi in range(b):
        xb = x[bi]
        for l in range(norm_w.shape[0]):
            xb = _layer(xb, c[bi:bi + 1], norm_w[l], w_ada[l], b_ada[l], w_in[l], mu[l], w0[l],
                        w_up[l], a0[l], a_up[l], k_k[l], k_a[l], r_k[l], gn_w[l], gn_b[l],
                        q_gain[l], k_gain[l], w_a_out[l], w_b_out[l], w_o[l])
        outs.append(xb)
    return jnp.stack(outs, axis=0)
```

```python
import functools

import numpy as np
import jax
import jax.numpy as jnp
from jax import lax
from jax.experimental import pallas as pl
from jax.experimental.pallas import tpu as pltpu

F32 = jnp.float32
BF16 = jnp.bfloat16
I32 = jnp.int32
HIGHEST = lax.Precision.HIGHEST

CHUNK = 64
HEAD_DIM = 64
N_HEADS = 8
D_BRANCH = N_HEADS * HEAD_DIM
LORA = 64
DECAY_SCALE = 0.606531
GN_EPS = 64e-5
NORM_EPS = 1e-6
TOPK_MAX = 256
Q_BLOCK = 128

LANES = 128
N_PAIRS = D_BRANCH // LANES
VMEM_LIMIT = 56 * 1024 * 1024

KEY_BLOCK = 1024
KEY_SUB = 256
COUNT_BLOCK = 512
INT_MIN = -(2 ** 31)
KEY_LOWEST_FINITE = -2139095040
NEG_BIG = -1e30
LOG2E = 1.4426950408889634

NA = 4 * D_BRANCH + 2 * LANES
NI = N_HEADS * 4 * HEAD_DIM + 4 * HEAD_DIM + LANES


def _sigmoid(x):
    return 1.0 / (1.0 + jnp.exp(-x))


def _hdot(a, b):
    return jnp.dot(a, b, precision=HIGHEST, preferred_element_type=F32)


def _split(x):
    hi = x.astype(BF16)
    lo = (x - hi.astype(F32)).astype(BF16)
    return hi, lo


def _dot(a, b, dims):
    return lax.dot_general(a, b, (dims, ((), ())), preferred_element_type=F32)


_NN = ((1,), (0,))
_NT = ((1,), (1,))
_TN = ((0,), (0,))


def _mm(sa, sb, dims=_NN):
    (ah, al), (bh, bl) = sa, sb
    ca, cb = dims[0][0], dims[1][0]
    return (_dot(jnp.concatenate([ah, al], axis=ca), jnp.concatenate([bh, bh], axis=cb), dims)
            + _dot(ah, bl, dims))


def _mm_exact_rhs(sa, b, dims=_NN):
    ah, al = sa
    ca, cb = dims[0][0], dims[1][0]
    return _dot(jnp.concatenate([ah, al], axis=ca), jnp.concatenate([b, b], axis=cb), dims)


def _mm_exact_lhs(a, sb, dims=_NN):
    bh, bl = sb
    ca, cb = dims[0][0], dims[1][0]
    return _dot(jnp.concatenate([a, a], axis=ca), jnp.concatenate([bh, bl], axis=cb), dims)


def _dot_nt(a, b):
    return lax.dot_general(a, b, (((1,), (1,)), ((), ())), preferred_element_type=F32)


def _iota(shape, dim):
    return lax.broadcasted_iota(I32, shape, dim)


def _ada_kernel(c_ref, w_ref, b_ref, o_ref):
    c = c_ref[...]
    s = c * _sigmoid(c)
    o_ref[...] = _hdot(s, w_ref[...]) + b_ref[...]


def _ada(c8, w_ada, b_ada):
    d = c8.shape[1]
    n = w_ada.shape[1]
    return pl.pallas_call(
        _ada_kernel,
        grid=(n // d,),
        in_specs=[pl.BlockSpec((8, d), lambda j: (0, 0)),
                  pl.BlockSpec((d, d), lambda j: (0, j)),
                  pl.BlockSpec((1, d), lambda j: (0, j))],
        out_specs=pl.BlockSpec((8, d), lambda j: (0, j)),
        out_shape=jax.ShapeDtypeStruct((8, n), F32),
        name="ada",
    )(c8, w_ada, b_ada)


def _inproj_kernel(x_ref, nw_ref, sc_ref, sh_ref, wa_ref, wq_ref, wi_ref, wg_ref,
                   pa_ref, pq_ref, pi_ref, pg_ref):
    x = x_ref[...]
    ms = jnp.mean(x * x, axis=-1, keepdims=True)
    h = x * lax.rsqrt(ms + NORM_EPS) * nw_ref[...]
    h = h * (1.0 + sc_ref[...]) + sh_ref[...]
    hb = h.astype(BF16)
    pa_ref[...] = jnp.dot(hb, wa_ref[...], preferred_element_type=F32)
    pq_ref[...] = jnp.dot(hb, wq_ref[...], preferred_element_type=F32)
    pi_ref[...] = jnp.dot(hb, wi_ref[...], preferred_element_type=F32)
    pg_ref[...] = jnp.dot(hb, wg_ref[...], preferred_element_type=F32)


def _inproj(x2, nw, scale, shift, wa, wq, wi, wg, tm):
    s, d = x2.shape
    row = lambda n: pl.BlockSpec((tm, n), lambda i: (i, 0))
    vec = pl.BlockSpec((1, d), lambda i: (0, 0))
    wsp = lambda w: pl.BlockSpec(w.shape, lambda i: (0, 0))
    return pl.pallas_call(
        _inproj_kernel,
        grid=(s // tm,),
        in_specs=[row(d), vec, vec, vec, wsp(wa), wsp(wq), wsp(wi), wsp(wg)],
        out_specs=[row(wa.shape[1]), row(wq.shape[1]), row(wi.shape[1]), row(wg.shape[1])],
        out_shape=[jax.ShapeDtypeStruct((s, w.shape[1]), F32) for w in (wa, wq, wi, wg)],
        compiler_params=pltpu.CompilerParams(dimension_semantics=("arbitrary",),
                                             vmem_limit_bytes=VMEM_LIMIT),
        name="inproj",
    )(x2, nw, scale, shift, wa, wq, wi, wg)


def _rwkv_kernel(pa_ref, mu_ref, w0_ref, a0_ref, kk_ref, ka_ref, rk_ref, gnw_ref, gnb_ref,
                 wup_ref, aup_ref, o_ref, s_ref, carry_ref, *, n_chunks):
    @pl.when(pl.program_id(0) == 0)
    def _():
        s_ref[...] = jnp.zeros_like(s_ref)
        carry_ref[...] = jnp.zeros_like(carry_ref)

    row = _iota((LANES, LANES), 0)
    col = _iota((LANES, LANES), 1)
    strict = row > col
    incl = row >= col
    same_head = (row < HEAD_DIM) == (col < HEAD_DIM)
    bd_ones = jnp.where(same_head, 1.0, 0.0).astype(BF16)
    eye = jnp.where(row == col, 1.0, 0.0).astype(F32)
    tri = jnp.where(_iota((CHUNK, CHUNK), 0) >= _iota((CHUNK, CHUNK), 1), 1.0, 0.0).astype(BF16)
    lane_h0 = _iota((CHUNK, LANES), 1) < HEAD_DIM
    first_row = _iota((CHUNK, NA), 0) == 0
    wup = _split(wup_ref[...])
    aup = _split(aup_ref[...])

    def stack(a):
        return jnp.concatenate([jnp.where(lane_h0, a, 0.0), jnp.where(lane_h0, 0.0, a)], axis=0)

    def chunk_body(c, carry):
        r0 = pl.multiple_of(c * CHUNK, CHUNK)
        x = pa_ref[pl.ds(r0, CHUNK), :]
        prev = pltpu.roll(x, 1, 0)
        prev = jnp.where(first_row, carry_ref[7:8, :], prev)
        carry_ref[...] = x[CHUNK - 8:CHUNK, :]
        xs = x + mu_ref[...] * (prev - x)
        d = D_BRANCH
        r = xs[:, 0:d]
        k = xs[:, d:2 * d]
        v = xs[:, 2 * d:3 * d]
        g = xs[:, 3 * d:4 * d]
        wd = xs[:, 4 * d:4 * d + LANES]
        ad = xs[:, 4 * d + LANES:4 * d + 2 * LANES]
        lw = -DECAY_SCALE * _sigmoid(w0_ref[...] + _mm(_split(jnp.tanh(wd)), wup))
        a = _sigmoid(a0_ref[...] + _mm(_split(ad), aup))
        kkx = k * kk_ref[...]
        k2 = k * (1.0 + (a - 1.0) * ka_ref[...])
        bon = r * k2 * rk_ref[...]

        sls = [slice(p * LANES, (p + 1) * LANES) for p in range(N_PAIRS)]
        pre = []
        for sl in sls:
            rp, kp, lwp, kkxp = r[:, sl], k2[:, sl], lw[:, sl], kkx[:, sl]
            ss = _mm_exact_rhs(_split(kkxp * kkxp), bd_ones)
            kkp = kkxp / jnp.maximum(jnp.sqrt(ss), 1e-12)
            alpha = -kkp
            beta = kkp * a[:, sl]
            cl = _mm_exact_lhs(tri, _split(lwp))
            clp = cl - lwp
            cm = cl[CHUNK // 2 - 1:CHUNK // 2, :]
            cend = cl[CHUNK - 1:CHUNK, :]
            e_k = jnp.exp(cm - cl)
            e_h = jnp.exp(cend - cl)
            lhs = _split(jnp.concatenate([stack(alpha * jnp.exp(clp - cm)),
                                          stack(rp * jnp.exp(cl - cm))], axis=0))
            rhs = _split(jnp.concatenate([stack(beta * e_k), stack(kp * e_k)], axis=0))
            pre.append(dict(
                gram=_mm(lhs, rhs, _NT),
                a_e=_split(stack(alpha * jnp.exp(clp))),
                r_e=_split(stack(rp * jnp.exp(cl))),
                bk=_split(jnp.concatenate([stack(beta * e_h), stack(kp * e_h)], axis=0)),
                v_st=_split(stack(v[:, sl])),
                g_end=jnp.exp(cend)))
        n_ab = [jnp.where(strict, q["gram"][0:LANES, 0:LANES], 0.0) for q in pre]
        tinv = [eye + n for n in n_ab]
        npow = n_ab
        for _ in range(5):
            sps = [_split(n) for n in npow]
            npow = [_mm(sp, sp) for sp in sps]
            tinv = [t + _mm(_split(t), _split(n)) for t, n in zip(tinv, npow)]
        s0 = [s_ref[p] for p in range(N_PAIRS)]
        s0s = [_split(x0) for x0 in s0]
        u_rhs = [_mm(q["a_e"], ss0, _NT)
                 + _mm(_split(jnp.where(strict, q["gram"][0:LANES, LANES:2 * LANES], 0.0)), q["v_st"])
                 for q, ss0 in zip(pre, s0s)]
        u_sp = [_split(_mm(_split(t), _split(u))) for t, u in zip(tinv, u_rhs)]
        for p, (q, us) in enumerate(zip(pre, u_sp)):
            uv = (jnp.concatenate([us[0], q["v_st"][0]], axis=0),
                  jnp.concatenate([us[1], q["v_st"][1]], axis=0))
            s_ref[p] = s0[p] * q["g_end"] + _mm(uv, q["bk"], _TN)
        ys = []
        for q, ss0, us in zip(pre, s0s, u_sp):
            m_rb = jnp.where(incl, q["gram"][LANES:2 * LANES, 0:LANES], 0.0)
            m_rk = jnp.where(incl, q["gram"][LANES:2 * LANES, LANES:2 * LANES], 0.0)
            y_st = _mm(q["r_e"], ss0, _NT) + _mm(_split(m_rb), us) + _mm(_split(m_rk), q["v_st"])
            ys.append(y_st[0:CHUNK] + y_st[CHUNK:2 * CHUNK])
        bonus = [_mm_exact_rhs(_split(bon[:, sl]), bd_ones) * v[:, sl] for sl in sls]
        dlt = [y - _mm_exact_rhs(_split(y), bd_ones) * (1.0 / HEAD_DIM) for y in ys]
        var = [_mm_exact_rhs(_split(dl * dl), bd_ones) * (1.0 / HEAD_DIM) for dl in dlt]
        for sl, dl, vr, bo in zip(sls, dlt, var, bonus):
            yn = dl * lax.rsqrt(vr + GN_EPS) * gnw_ref[:, sl] + gnb_ref[:, sl]
            gp = g[:, sl]
            o_ref[pl.ds(r0, CHUNK), sl] = (yn + bo) * (gp * _sigmoid(gp))
        return carry

    lax.fori_loop(0, n_chunks, chunk_body, 0)


def _rwkv(pa, mu, w0, a0, k_k, k_a, r_k, gn_w, gn_b, w_up, a_up, tm):
    s = pa.shape[0]
    vec = lambda n: pl.BlockSpec((1, n), lambda i: (0, 0))
    lora = pl.BlockSpec((LANES, D_BRANCH), lambda i: (0, 0))
    return pl.pallas_call(
        functools.partial(_rwkv_kernel, n_chunks=tm // CHUNK),
        grid=(s // tm,),
        in_specs=[pl.BlockSpec((tm, NA), lambda i: (i, 0)), vec(NA)] + [vec(D_BRANCH)] * 7
                 + [lora, lora],
        out_specs=pl.BlockSpec((tm, D_BRANCH), lambda i: (i, 0)),
        out_shape=jax.ShapeDtypeStruct((s, D_BRANCH), F32),
        scratch_shapes=[pltpu.VMEM((N_PAIRS, LANES, LANES), F32),
                        pltpu.VMEM((8, NA), F32)],
        compiler_params=pltpu.CompilerParams(dimension_semantics=("arbitrary",),
                                             vmem_limit_bytes=VMEM_LIMIT),
        name="rwkv",
    )(pa, mu, w0, a0, k_k, k_a, r_k, gn_w, gn_b, w_up, a_up)


def _split_hi_lo(x, take_hi):
    hi = x.astype(BF16)
    lo = (x - hi.astype(F32)).astype(BF16)
    return jnp.where(take_hi, hi, lo)


def _dprep_kernel(pq_ref, pi_ref, qg_ref, kg_ref, qa_ref, qb_ref, k_ref, v_ref,
                  iq_ref, ik_ref, iw_ref):
    tm = pq_ref.shape[0]
    d = D_BRANCH
    row = _iota((LANES, LANES), 0)
    col = _iota((LANES, LANES), 1)
    bd_avg = jnp.where((row < HEAD_DIM) == (col < HEAD_DIM), 1.0 / HEAD_DIM, 0.0).astype(F32)
    lane_h0 = _iota((tm, LANES), 1) < HEAD_DIM
    q_scale = (HEAD_DIM ** -0.5) * LOG2E
    for p in range(N_PAIRS):
        sl = slice(p * LANES, (p + 1) * LANES)
        q = pq_ref[:, p * LANES:(p + 1) * LANES]
        k = pq_ref[:, d + p * LANES:d + (p + 1) * LANES]
        qn = q * lax.rsqrt(_hdot(q * q, bd_avg) + NORM_EPS) * qg_ref[:, sl]
        kn = k * lax.rsqrt(_hdot(k * k, bd_avg) + NORM_EPS) * kg_ref[:, sl]
        qs = (qn * q_scale).astype(BF16)
        qa_ref[:, sl] = jnp.where(lane_h0, qs, jnp.zeros_like(qs))
        qb_ref[:, sl] = jnp.where(lane_h0, jnp.zeros_like(qs), qs)
        k_ref[:, sl] = kn.astype(BF16)
    v_ref[...] = pq_ref[:, 2 * d:3 * d].astype(BF16)
    lane4 = _iota((tm, 4 * HEAD_DIM), 1) // HEAD_DIM
    q_hi = (lane4 % 2) == 0
    k_hi = lane4 < 2
    off = N_HEADS * 4 * HEAD_DIM
    iw = pi_ref[:, off + 4 * HEAD_DIM:off + 4 * HEAD_DIM + LANES] * (
        (N_HEADS ** -0.5) * (HEAD_DIM ** -0.5))
    for h in range(N_HEADS):
        sl = slice(h * 4 * HEAD_DIM, (h + 1) * 4 * HEAD_DIM)
        iq_ref[h] = _split_hi_lo(pi_ref[:, sl], q_hi)
        iw_ref[h] = jnp.broadcast_to(iw[:, h:h + 1], (tm, LANES))
    ik_ref[...] = _split_hi_lo(pi_ref[:, off:off + 4 * HEAD_DIM], k_hi)


def _dprep(pq, pi, qg, kg, tm):
    s = pq.shape[0]
    row = lambda n: pl.BlockSpec((tm, n), lambda i: (i, 0))
    hrow = lambda n: pl.BlockSpec((N_HEADS, tm, n), lambda i: (0, i, 0))
    vec = pl.BlockSpec((1, D_BRANCH), lambda i: (0, 0))
    kq = 4 * HEAD_DIM
    return pl.pallas_call(
        _dprep_kernel,
        grid=(s // tm,),
        in_specs=[row(pq.shape[1]), row(pi.shape[1]), vec, vec],
        out_specs=[row(D_BRANCH)] * 4 + [hrow(kq), row(kq), hrow(LANES)],
        out_shape=[jax.ShapeDtypeStruct((s, D_BRANCH), BF16)] * 4
                  + [jax.ShapeDtypeStruct((N_HEADS, s, kq), BF16),
                     jax.ShapeDtypeStruct((s, kq), BF16),
                     jax.ShapeDtypeStruct((N_HEADS, s, LANES), F32)],
        compiler_params=pltpu.CompilerParams(dimension_semantics=("arbitrary",),
                                             vmem_limit_bytes=VMEM_LIMIT),
        name="dprep",
    )(pq, pi, qg, kg)


def _block_tables(s):
    qb_l, kb_l, nkb_l = [], [], []
    for qb in range(s // Q_BLOCK):
        nkb = -(-(Q_BLOCK * (qb + 1)) // KEY_BLOCK)
        for kb in range(nkb):
            qb_l.append(qb)
            kb_l.append(kb)
            nkb_l.append(nkb)
    return (np.asarray(qb_l, np.int32), np.asarray(kb_l, np.int32), np.asarray(nkb_l, np.int32))


def _key_to_float(key):
    bits = key ^ ((key >> 31) & 0x7FFFFFFF)
    return pltpu.bitcast(bits, F32)


def _select_kernel(qb_tab, kb_tab, nkb_tab, iq_ref, iw_ref, ik_ref, mask_ref, sc_ref, *,
                   topk, seq):
    step = pl.program_id(0)
    qb = qb_tab[step]
    kb = kb_tab[step]
    nkb = nkb_tab[step]
    rows = Q_BLOCK

    q_chunk = (qb * Q_BLOCK + _iota((rows, KEY_SUB), 0)) // CHUNK
    lane_pos = _iota((rows, KEY_SUB), 1)
    iq_all = iq_ref[...].reshape(N_HEADS * rows, 4 * HEAD_DIM)
    for kc in range(KEY_BLOCK // KEY_SUB):
        sc = _dot_nt(iq_all, ik_ref[kc * KEY_SUB:(kc + 1) * KEY_SUB, :])
        acc = jnp.zeros((rows, KEY_SUB), F32)
        for h in range(N_HEADS):
            w = iw_ref[h]
            acc = acc + jnp.maximum(sc[h * rows:(h + 1) * rows], 0.0) * jnp.concatenate(
                [w] * (KEY_SUB // LANES), axis=1)
        base = kb * KEY_BLOCK + kc * KEY_SUB
        adm = ((base + lane_pos) // CHUNK) <= q_chunk
        sc_ref[:, pl.ds(pl.multiple_of(base, KEY_SUB), KEY_SUB)] = jnp.where(adm, acc, -jnp.inf)

    @pl.when(kb == nkb - 1)
    def _():
        n_cnt = nkb * (KEY_BLOCK // COUNT_BLOCK)

        def count(pred):
            def body(j, part):
                c0 = pl.multiple_of(j * COUNT_BLOCK, COUNT_BLOCK)
                blk = sc_ref[:, pl.ds(c0, COUNT_BLOCK)]
                for i in range(COUNT_BLOCK // LANES):
                    hit = pred(blk[:, i * LANES:(i + 1) * LANES], c0 + i * LANES)
                    part = part + jnp.where(hit, 1, 0)
                return part
            part = lax.fori_loop(0, n_cnt, body, jnp.zeros((rows, LANES), I32))
            return jnp.sum(part, axis=1, keepdims=True)

        def count_ge(key):
            cb = jnp.broadcast_to(_key_to_float(key), (rows, LANES))
            return count(lambda blk, c0: blk >= cb)

        zero = jnp.zeros((rows, 1), I32)
        t = jnp.where(count_ge(zero) >= topk, zero, jnp.full((rows, 1), INT_MIN, I32))

        def bit_body(i, t):
            cand = t | jnp.left_shift(jnp.int32(1), 30 - i)
            return jnp.where(count_ge(cand) >= topk, cand, t)

        t = lax.fori_loop(0, 31, bit_body, t)
        tau = _key_to_float(jnp.maximum(t, KEY_LOWEST_FINITE))
        tau_b = jnp.broadcast_to(tau, (rows, LANES))
        n_ge = count(lambda blk, c0: blk >= tau_b)
        excess = jnp.max(n_ge) > topk

        def write_mask(sel):
            def body(j, carry):
                c0 = pl.multiple_of(j * COUNT_BLOCK, COUNT_BLOCK)
                blk = sc_ref[:, pl.ds(c0, COUNT_BLOCK)]
                out = jnp.concatenate(
                    [jnp.where(sel(blk[:, i * LANES:(i + 1) * LANES], c0 + i * LANES), 1.0, 0.0)
                     for i in range(COUNT_BLOCK // LANES)], axis=1)
                mask_ref[:, pl.ds(c0, COUNT_BLOCK)] = out.astype(jnp.int8)
                return carry
            lax.fori_loop(0, n_cnt, body, 0)

        @pl.when(jnp.logical_not(excess))
        def _():
            write_mask(lambda blk, c0: blk >= tau_b)

        @pl.when(excess)
        def _():
            n_gt = count(lambda blk, c0: blk > tau_b)
            need = topk - n_gt
            lane = _iota((rows, LANES), 1)

            def count_tie_upto(cut):
                cutb = jnp.broadcast_to(cut, (rows, LANES))
                return count(lambda blk, c0: (blk == tau_b) & ((c0 + lane) <= cutb))

            n_bits = int(seq - 1).bit_length()

            def pos_body(i, cut):
                cand = cut - jnp.left_shift(jnp.int32(1), n_bits - 1 - i)
                ok = count_tie_upto(cand) >= need
                return jnp.where(ok, cand, cut)

            cut0 = jnp.full((rows, 1), (1 << n_bits) - 1, I32)
            cut = lax.fori_loop(0, n_bits, pos_body, cut0)
            cutb = jnp.broadcast_to(cut, (rows, LANES))
            write_mask(lambda blk, c0: (blk > tau_b) | ((blk == tau_b) & ((c0 + lane) <= cutb)))

        def zero_body(j, carry):
            c0 = pl.multiple_of(j * COUNT_BLOCK, COUNT_BLOCK)
            mask_ref[:, pl.ds(c0, COUNT_BLOCK)] = jnp.zeros((rows, COUNT_BLOCK), jnp.int8)
            return carry
        lax.fori_loop(n_cnt, seq // COUNT_BLOCK, zero_body, 0)


def _select(iq, iw, ik, topk):
    s = ik.shape[0]
    qb_t, kb_t, nkb_t = _block_tables(s)
    grid_spec = pltpu.PrefetchScalarGridSpec(
        num_scalar_prefetch=3,
        grid=(len(qb_t),),
        in_specs=[pl.BlockSpec((N_HEADS, Q_BLOCK, iq.shape[2]), lambda i, qb, kb, nk: (0, qb[i], 0)),
                  pl.BlockSpec((N_HEADS, Q_BLOCK, LANES), lambda i, qb, kb, nk: (0, qb[i], 0)),
                  pl.BlockSpec((KEY_BLOCK, ik.shape[1]), lambda i, qb, kb, nk: (kb[i], 0))],
        out_specs=pl.BlockSpec((Q_BLOCK, s), lambda i, qb, kb, nk: (qb[i], 0)),
        scratch_shapes=[pltpu.VMEM((Q_BLOCK, s), F32)],
    )
    return pl.pallas_call(
        functools.partial(_select_kernel, topk=topk, seq=s),
        grid_spec=grid_spec,
        out_shape=jax.ShapeDtypeStruct((s, s), jnp.int8),
        compiler_params=pltpu.CompilerParams(dimension_semantics=("arbitrary",),
                                             vmem_limit_bytes=VMEM_LIMIT),
        name="select",
    )(jnp.asarray(qb_t), jnp.asarray(kb_t), jnp.asarray(nkb_t), iq, iw, ik)


def _attn_kernel(qb_tab, kb_tab, nkb_tab, qa_ref, qb_ref, k_ref, v_ref, mask_ref, o_ref,
                 m_ref, acc_ref, bias_ref, s_ref):
    step = pl.program_id(0)
    kb = kb_tab[step]
    nkb = nkb_tab[step]
    n_sub = KEY_BLOCK // KEY_SUB

    @pl.when(kb == 0)
    def _():
        m_ref[...] = jnp.full(m_ref.shape, NEG_BIG, F32)
        acc_ref[...] = jnp.zeros_like(acc_ref)

    for kc in range(n_sub):
        ks = slice(kc * KEY_SUB, (kc + 1) * KEY_SUB)
        bias_ref[:, ks] = (mask_ref[:, ks].astype(F32) - 1.0) * (-NEG_BIG)

    ones = jnp.ones((KEY_SUB, LANES), BF16)
    q_refs = (qa_ref, qb_ref)
    lanes_of = lambda h: slice((h // 2) * LANES, (h // 2 + 1) * LANES)
    m_old, m_new = [], []
    for h in range(N_HEADS):
        sl = lanes_of(h)
        q = q_refs[h % 2][:, sl]
        bm = None
        for kc in range(n_sub):
            ks = slice(kc * KEY_SUB, (kc + 1) * KEY_SUB)
            s = _dot_nt(q, k_ref[ks, sl]) + bias_ref[:, ks]
            s_ref[h, :, ks] = s
            mx = jnp.max(s, axis=1, keepdims=True)
            bm = mx if bm is None else jnp.maximum(bm, mx)
        m_old.append(m_ref[h])
        m_new.append(jnp.maximum(m_old[h], bm))
    for h in range(N_HEADS):
        sl = lanes_of(h)
        m2 = jnp.concatenate([m_new[h]] * (KEY_SUB // LANES), axis=1)
        pv = None
        for kc in range(n_sub):
            ks = slice(kc * KEY_SUB, (kc + 1) * KEY_SUB)
            prob = jnp.exp2(s_ref[h, :, ks] - m2).astype(BF16)
            v_aug = jnp.concatenate([v_ref[ks, sl], ones], axis=1)
            d = jnp.dot(prob, v_aug, preferred_element_type=F32)
            pv = d if pv is None else pv + d
        alpha = jnp.exp2(m_old[h] - m_new[h])
        acc_ref[h] = acc_ref[h] * jnp.concatenate([alpha, alpha], axis=1) + pv
        m_ref[h] = m_new[h]

    @pl.when(kb == nkb - 1)
    def _():
        lane_h0 = _iota((Q_BLOCK, LANES), 1) < HEAD_DIM
        for p in range(N_PAIRS):
            sl = slice(p * LANES, (p + 1) * LANES)
            a0 = acc_ref[2 * p]
            a1 = acc_ref[2 * p + 1]
            o_ref[:, sl] = jnp.where(lane_h0, a0[:, 0:LANES] / a0[:, LANES:2 * LANES],
                                     a1[:, 0:LANES] / a1[:, LANES:2 * LANES])


def _attn(qa, qb, k, v, mask):
    s = k.shape[0]
    qb_t, kb_t, nkb_t = _block_tables(s)
    qspec = pl.BlockSpec((Q_BLOCK, D_BRANCH), lambda i, qb, kb, nk: (qb[i], 0))
    kspec = pl.BlockSpec((KEY_BLOCK, D_BRANCH), lambda i, qb, kb, nk: (kb[i], 0))
    grid_spec = pltpu.PrefetchScalarGridSpec(
        num_scalar_prefetch=3,
        grid=(len(qb_t),),
        in_specs=[qspec, qspec, kspec, kspec,
                  pl.BlockSpec((Q_BLOCK, KEY_BLOCK), lambda i, qb, kb, nk: (qb[i], kb[i]))],
        out_specs=qspec,
        scratch_shapes=[pltpu.VMEM((N_HEADS, Q_BLOCK, LANES), F32),
                        pltpu.VMEM((N_HEADS, Q_BLOCK, 2 * LANES), F32),
                        pltpu.VMEM((Q_BLOCK, KEY_BLOCK), F32),
                        pltpu.VMEM((N_HEADS, Q_BLOCK, KEY_BLOCK), F32)],
    )
    return pl.pallas_call(
        _attn_kernel,
        grid_spec=grid_spec,
        out_shape=jax.ShapeDtypeStruct((s, D_BRANCH), F32),
        compiler_params=pltpu.CompilerParams(dimension_semantics=("arbitrary",),
                                             vmem_limit_bytes=VMEM_LIMIT),
        name="attn",
    )(jnp.asarray(qb_t), jnp.asarray(kb_t), jnp.asarray(nkb_t), qa, qb, k, v, mask)


def _outproj_kernel(x_ref, za_ref, yb_ref, gb_ref, pg_ref, gate_ref, wa_ref, wb_ref, wo_ref,
                    o_ref):
    d = x_ref.shape[1]
    gb = gb_ref[...]
    zb = yb_ref[...] * (gb * _sigmoid(gb))
    ya = jnp.dot(za_ref[...].astype(BF16), wa_ref[...], preferred_element_type=F32)
    yb = jnp.dot(zb.astype(BF16), wb_ref[...], preferred_element_type=F32)
    merged = _sigmoid(pg_ref[:, 0:d]) * ya + _sigmoid(pg_ref[:, d:2 * d]) * yb
    out = jnp.dot(merged.astype(BF16), wo_ref[...], preferred_element_type=F32)
    o_ref[...] = x_ref[...] + gate_ref[...] * out


def _outproj(x2, za, yb, pq, pg, gate, wa, wb, wo, tm):
    s, d = x2.shape
    row = lambda n: pl.BlockSpec((tm, n), lambda i: (i, 0))
    wsp = lambda w: pl.BlockSpec(w.shape, lambda i: (0, 0))
    gb_col = 3 * D_BRANCH // D_BRANCH
    return pl.pallas_call(
        _outproj_kernel,
        grid=(s // tm,),
        in_specs=[row(d), row(D_BRANCH), row(D_BRANCH),
                  pl.BlockSpec((tm, D_BRANCH), lambda i: (i, gb_col)),
                  row(2 * d), pl.BlockSpec((1, d), lambda i: (0, 0)),
                  wsp(wa), wsp(wb), wsp(wo)],
        out_specs=row(d),
        out_shape=jax.ShapeDtypeStruct((s, d), F32),
        compiler_params=pltpu.CompilerParams(dimension_semantics=("arbitrary",),
                                             vmem_limit_bytes=VMEM_LIMIT),
        name="outproj",
    )(x2, za, yb, pq, pg, gate, wa, wb, wo)


def _pad_cols(w, n):
    return jnp.pad(w, ((0, 0), (0, n - w.shape[1])))


def _layer(x2, c, norm_w, w_ada, b_ada, w_in, mu, w0, w_up, a0, a_up, k_k, k_a, r_k, gn_w,
           gn_b, q_gain, k_gain, w_a_out, w_b_out, w_o):
    s, d = x2.shape
    db = D_BRANCH
    assert s % KEY_BLOCK == 0 and d % LANES == 0
    topk = min(TOPK_MAX, s // 4)

    mod = _ada(jnp.broadcast_to(c, (8, d)), w_ada, b_ada[None, :])[0:1]
    shift, scale, gate = mod[:, 0:d], mod[:, d:2 * d], mod[:, 2 * d:3 * d]

    n_rwkv = 4 * db + 2 * LORA
    def rwkv_cols(w):
        return jnp.concatenate(
            [w[:, 0:3 * db], w[:, 3 * db + 2 * LORA:n_rwkv],
             _pad_cols(w[:, 3 * db:3 * db + LORA], LANES),
             _pad_cols(w[:, 3 * db + LORA:3 * db + 2 * LORA], LANES)], axis=1)
    wa = rwkv_cols(w_in[:, 0:n_rwkv]).astype(BF16)
    mu_a = rwkv_cols(mu[None, :])
    o = n_rwkv
    wq = w_in[:, o:o + 4 * db].astype(BF16)
    o += 4 * db
    w_iq = w_in[:, o:o + N_HEADS * HEAD_DIM].reshape(d, N_HEADS, 1, HEAD_DIM)
    w_iq = jnp.broadcast_to(w_iq, (d, N_HEADS, 4, HEAD_DIM)).reshape(d, N_HEADS * 4 * HEAD_DIM)
    o += N_HEADS * HEAD_DIM
    w_ik = jnp.tile(w_in[:, o:o + HEAD_DIM], (1, 4))
    o += HEAD_DIM
    w_iw = _pad_cols(w_in[:, o:o + N_HEADS], LANES)
    o += N_HEADS
    wi = jnp.concatenate([w_iq, w_ik, w_iw], axis=1).astype(BF16)
    wg = w_in[:, o:o + 2 * d].astype(BF16)

    pa, pq, pi, pg = _inproj(x2, norm_w[None, :], scale, shift, wa, wq, wi, wg, tm=256)

    pad_rows = lambda w: jnp.pad(w, ((0, LANES - w.shape[0]), (0, 0)))
    za = _rwkv(pa, mu_a, w0[None, :], a0[None, :], k_k[None, :], k_a[None, :],
               r_k.reshape(1, db), gn_w[None, :], gn_b[None, :], pad_rows(w_up), pad_rows(a_up),
               tm=512)

    qg = jnp.tile(q_gain, N_HEADS)[None, :]
    kg = jnp.tile(k_gain, N_HEADS)[None, :]
    qa, qb, kn, vb, iq, ik, iw = _dprep(pq, pi, qg, kg, tm=512)
    mask = _select(iq, iw, ik, topk)
    yb = _attn(qa, qb, kn, vb, mask)

    return _outproj(x2, za, yb, pq, pg, gate, w_a_out.astype(BF16), w_b_out.astype(BF16),
                    w_o.astype(BF16), tm=512)


def kernel(x, c, norm_w, w_ada, b_ada, w_in, mu, w0, w_up, a0, a_up, k_k, k_a, r_k, gn_w, gn_b,
           q_gain, k_gain, w_a_out, w_b_out, w_o):
    b, s, d = x.shape
    outs = []
    for bi in range(b):
        xb = x[bi]
        for l in range(norm_w.shape[0]):
            xb = _layer(xb, c[bi:bi + 1], norm_w[l], w_ada[l], b_ada[l], w_in[l], mu[l], w0[l],
                        w_up[l], a0[l], a_up[l], k_k[l], k_a[l], r_k[l], gn_w[l], gn_b[l],
                        q_gain[l], k_gain[l], w_a_out[l], w_b_out[l], w_o[l])
        outs.append(xb)
    return jnp.stack(outs, axis=0)
```

```python
import functools

import numpy as np
import jax
import jax.numpy as jnp
from jax import lax
from jax.experimental import pallas as pl
from jax.experimental.pallas import tpu as pltpu

F32 = jnp.float32
BF16 = jnp.bfloat16
I32 = jnp.int32
HIGHEST = lax.Precision.HIGHEST

CHUNK = 64
HEAD_DIM = 64
N_HEADS = 8
D_BRANCH = N_HEADS * HEAD_DIM
LORA = 64
DECAY_SCALE = 0.606531
GN_EPS = 64e-5
NORM_EPS = 1e-6
TOPK_MAX = 256
Q_BLOCK = 256

LANES = 128
N_PAIRS = D_BRANCH // LANES
VMEM_LIMIT = 56 * 1024 * 1024

KEY_BLOCK = 1024
KEY_SUB = 256
ROW_SUB = 128
COUNT_BLOCK = 512
VALUE_ROUNDS = 20
BRACKET_SLACK = 2.0 ** -10
FLT_MAX = 3.4028234663852886e38
MIN_NORMAL_BITS = 0x00800000
NEG_BIG = -1e30
LOG2E = 1.4426950408889634

NA = 4 * D_BRANCH + 2 * LANES
NI = N_HEADS * 4 * HEAD_DIM + 4 * HEAD_DIM + LANES


def _sigmoid(x):
    return 1.0 / (1.0 + jnp.exp(-x))


def _hdot(a, b):
    return jnp.dot(a, b, precision=HIGHEST, preferred_element_type=F32)


def _split(x):
    hi = x.astype(BF16)
    lo = (x - hi.astype(F32)).astype(BF16)
    return hi, lo


def _dot(a, b, dims):
    return lax.dot_general(a, b, (dims, ((), ())), preferred_element_type=F32)


_NN = ((1,), (0,))
_NT = ((1,), (1,))
_TN = ((0,), (0,))


def _mm(sa, sb, dims=_NN):
    (ah, al), (bh, bl) = sa, sb
    ca, cb = dims[0][0], dims[1][0]
    return (_dot(jnp.concatenate([ah, al], axis=ca), jnp.concatenate([bh, bh], axis=cb), dims)
            + _dot(ah, bl, dims))


def _mm_exact_rhs(sa, b, dims=_NN):
    ah, al = sa
    ca, cb = dims[0][0], dims[1][0]
    return _dot(jnp.concatenate([ah, al], axis=ca), jnp.concatenate([b, b], axis=cb), dims)


def _mm_exact_lhs(a, sb, dims=_NN):
    bh, bl = sb
    ca, cb = dims[0][0], dims[1][0]
    return _dot(jnp.concatenate([a, a], axis=ca), jnp.concatenate([bh, bl], axis=cb), dims)


def _dot_nt(a, b):
    return lax.dot_general(a, b, (((1,), (1,)), ((), ())), preferred_element_type=F32)


def _iota(shape, dim):
    return lax.broadcasted_iota(I32, shape, dim)


def _ada_kernel(c_ref, w_ref, b_ref, o_ref):
    c = c_ref[...]
    s = c * _sigmoid(c)
    o_ref[...] = _hdot(s, w_ref[...]) + b_ref[...]


def _ada(c8, w_ada, b_ada):
    d = c8.shape[1]
    n = w_ada.shape[1]
    return pl.pallas_call(
        _ada_kernel,
        grid=(n // d,),
        in_specs=[pl.BlockSpec((8, d), lambda j: (0, 0)),
                  pl.BlockSpec((d, d), lambda j: (0, j)),
                  pl.BlockSpec((1, d), lambda j: (0, j))],
        out_specs=pl.BlockSpec((8, d), lambda j: (0, j)),
        out_shape=jax.ShapeDtypeStruct((8, n), F32),
        name="ada",
    )(c8, w_ada, b_ada)


def _inproj_kernel(x_ref, nw_ref, sc_ref, sh_ref, wa_ref, wq_ref, wi_ref, wg_ref,
                   pa_ref, pq_ref, pi_ref, pg_ref):
    x = x_ref[...]
    ms = jnp.mean(x * x, axis=-1, keepdims=True)
    h = x * lax.rsqrt(ms + NORM_EPS) * nw_ref[...]
    h = h * (1.0 + sc_ref[...]) + sh_ref[...]
    hb = h.astype(BF16)
    pa_ref[...] = jnp.dot(hb, wa_ref[...], preferred_element_type=F32)
    pq_ref[...] = jnp.dot(hb, wq_ref[...], preferred_element_type=F32)
    pi_ref[...] = jnp.dot(hb, wi_ref[...], preferred_element_type=F32)
    pg_ref[...] = jnp.dot(hb, wg_ref[...], preferred_element_type=F32)


def _inproj(x2, nw, scale, shift, wa, wq, wi, wg, tm):
    s, d = x2.shape
    row = lambda n: pl.BlockSpec((tm, n), lambda i: (i, 0))
    vec = pl.BlockSpec((1, d), lambda i: (0, 0))
    wsp = lambda w: pl.BlockSpec(w.shape, lambda i: (0, 0))
    return pl.pallas_call(
        _inproj_kernel,
        grid=(s // tm,),
        in_specs=[row(d), vec, vec, vec, wsp(wa), wsp(wq), wsp(wi), wsp(wg)],
        out_specs=[row(wa.shape[1]), row(wq.shape[1]), row(wi.shape[1]), row(wg.shape[1])],
        out_shape=[jax.ShapeDtypeStruct((s, w.shape[1]), F32) for w in (wa, wq, wi, wg)],
        compiler_params=pltpu.CompilerParams(dimension_semantics=("arbitrary",),
                                             vmem_limit_bytes=VMEM_LIMIT),
        name="inproj",
    )(x2, nw, scale, shift, wa, wq, wi, wg)


def _rwkv_kernel(pa_ref, mu_ref, w0_ref, a0_ref, kk_ref, ka_ref, rk_ref, gnw_ref, gnb_ref,
                 wup_ref, aup_ref, o_ref, s_ref, carry_ref, *, n_chunks):
    @pl.when(pl.program_id(0) == 0)
    def _():
        s_ref[...] = jnp.zeros_like(s_ref)
        carry_ref[...] = jnp.zeros_like(carry_ref)

    row = _iota((LANES, LANES), 0)
    col = _iota((LANES, LANES), 1)
    strict = row > col
    incl = row >= col
    same_head = (row < HEAD_DIM) == (col < HEAD_DIM)
    bd_ones = jnp.where(same_head, 1.0, 0.0).astype(BF16)
    eye = jnp.where(row == col, 1.0, 0.0).astype(F32)
    tri = jnp.where(_iota((CHUNK, CHUNK), 0) >= _iota((CHUNK, CHUNK), 1), 1.0, 0.0).astype(BF16)
    lane_h0 = _iota((CHUNK, LANES), 1) < HEAD_DIM
    first_row = _iota((CHUNK, NA), 0) == 0
    wup = _split(wup_ref[...])
    aup = _split(aup_ref[...])

    def stack(a):
        return jnp.concatenate([jnp.where(lane_h0, a, 0.0), jnp.where(lane_h0, 0.0, a)], axis=0)

    def chunk_body(c, carry):
        r0 = pl.multiple_of(c * CHUNK, CHUNK)
        x = pa_ref[pl.ds(r0, CHUNK), :]
        prev = pltpu.roll(x, 1, 0)
        prev = jnp.where(first_row, carry_ref[7:8, :], prev)
        carry_ref[...] = x[CHUNK - 8:CHUNK, :]
        xs = x + mu_ref[...] * (prev - x)
        d = D_BRANCH
        r = xs[:, 0:d]
        k = xs[:, d:2 * d]
        v = xs[:, 2 * d:3 * d]
        g = xs[:, 3 * d:4 * d]
        wd = xs[:, 4 * d:4 * d + LANES]
        ad = xs[:, 4 * d + LANES:4 * d + 2 * LANES]
        lw = -DECAY_SCALE * _sigmoid(w0_ref[...] + _mm(_split(jnp.tanh(wd)), wup))
        a = _sigmoid(a0_ref[...] + _mm(_split(ad), aup))
        kkx = k * kk_ref[...]
        k2 = k * (1.0 + (a - 1.0) * ka_ref[...])
        bon = r * k2 * rk_ref[...]

        sls = [slice(p * LANES, (p + 1) * LANES) for p in range(N_PAIRS)]
        pre = []
        for sl in sls:
            rp, kp, lwp, kkxp = r[:, sl], k2[:, sl], lw[:, sl], kkx[:, sl]
            ss = _mm_exact_rhs(_split(kkxp * kkxp), bd_ones)
            kkp = kkxp / jnp.maximum(jnp.sqrt(ss), 1e-12)
            alpha = -kkp
            beta = kkp * a[:, sl]
            cl = _mm_exact_lhs(tri, _split(lwp))
            clp = cl - lwp
            cm = cl[CHUNK // 2 - 1:CHUNK // 2, :]
            cend = cl[CHUNK - 1:CHUNK, :]
            e_k = jnp.exp(cm - cl)
            e_h = jnp.exp(cend - cl)
            lhs = _split(jnp.concatenate([stack(alpha * jnp.exp(clp - cm)),
                                          stack(rp * jnp.exp(cl - cm))], axis=0))
            rhs = _split(jnp.concatenate([stack(beta * e_k), stack(kp * e_k)], axis=0))
            pre.append(dict(
                gram=_mm(lhs, rhs, _NT),
                a_e=_split(stack(alpha * jnp.exp(clp))),
                r_e=_split(stack(rp * jnp.exp(cl))),
                bk=_split(jnp.concatenate([stack(beta * e_h), stack(kp * e_h)], axis=0)),
                v_st=_split(stack(v[:, sl])),
                g_end=jnp.exp(cend)))
        n_ab = [jnp.where(strict, q["gram"][0:LANES, 0:LANES], 0.0) for q in pre]
        tinv = [eye + n for n in n_ab]
        npow = n_ab
        for _ in range(5):
            sps = [_split(n) for n in npow]
            npow = [_mm(sp, sp) for sp in sps]
            tinv = [t + _mm(_split(t), _split(n)) for t, n in zip(tinv, npow)]
        s0 = [s_ref[p] for p in range(N_PAIRS)]
        s0s = [_split(x0) for x0 in s0]
        u_rhs = [_mm(q["a_e"], ss0, _NT)
                 + _mm(_split(jnp.where(strict, q["gram"][0:LANES, LANES:2 * LANES], 0.0)), q["v_st"])
                 for q, ss0 in zip(pre, s0s)]
        u_sp = [_split(_mm(_split(t), _split(u))) for t, u in zip(tinv, u_rhs)]
        for p, (q, us) in enumerate(zip(pre, u_sp)):
            uv = (jnp.concatenate([us[0], q["v_st"][0]], axis=0),
                  jnp.concatenate([us[1], q["v_st"][1]], axis=0))
            s_ref[p] = s0[p] * q["g_end"] + _mm(uv, q["bk"], _TN)
        ys = []
        for q, ss0, us in zip(pre, s0s, u_sp):
            m_rb = jnp.where(incl, q["gram"][LANES:2 * LANES, 0:LANES], 0.0)
            m_rk = jnp.where(incl, q["gram"][LANES:2 * LANES, LANES:2 * LANES], 0.0)
            y_st = _mm(q["r_e"], ss0, _NT) + _mm(_split(m_rb), us) + _mm(_split(m_rk), q["v_st"])
            ys.append(y_st[0:CHUNK] + y_st[CHUNK:2 * CHUNK])
        bonus = [_mm_exact_rhs(_split(bon[:, sl]), bd_ones) * v[:, sl] for sl in sls]
        dlt = [y - _mm_exact_rhs(_split(y), bd_ones) * (1.0 / HEAD_DIM) for y in ys]
        var = [_mm_exact_rhs(_split(dl * dl), bd_ones) * (1.0 / HEAD_DIM) for dl in dlt]
        for sl, dl, vr, bo in zip(sls, dlt, var, bonus):
            yn = dl * lax.rsqrt(vr + GN_EPS) * gnw_ref[:, sl] + gnb_ref[:, sl]
            gp = g[:, sl]
            o_ref[pl.ds(r0, CHUNK), sl] = (yn + bo) * (gp * _sigmoid(gp))
        return carry

    lax.fori_loop(0, n_chunks, chunk_body, 0)


def _rwkv(pa, mu, w0, a0, k_k, k_a, r_k, gn_w, gn_b, w_up, a_up, tm):
    s = pa.shape[0]
    vec = lambda n: pl.BlockSpec((1, n), lambda i: (0, 0))
    lora = pl.BlockSpec((LANES, D_BRANCH), lambda i: (0, 0))
    return pl.pallas_call(
        functools.partial(_rwkv_kernel, n_chunks=tm // CHUNK),
        grid=(s // tm,),
        in_specs=[pl.BlockSpec((tm, NA), lambda i: (i, 0)), vec(NA)] + [vec(D_BRANCH)] * 7
                 + [lora, lora],
        out_specs=pl.BlockSpec((tm, D_BRANCH), lambda i: (i, 0)),
        out_shape=jax.ShapeDtypeStruct((s, D_BRANCH), F32),
        scratch_shapes=[pltpu.VMEM((N_PAIRS, LANES, LANES), F32),
                        pltpu.VMEM((8, NA), F32)],
        compiler_params=pltpu.CompilerParams(dimension_semantics=("arbitrary",),
                                             vmem_limit_bytes=VMEM_LIMIT),
        name="rwkv",
    )(pa, mu, w0, a0, k_k, k_a, r_k, gn_w, gn_b, w_up, a_up)


def _split_hi_lo(x, take_hi):
    hi = x.astype(BF16)
    lo = (x - hi.astype(F32)).astype(BF16)
    return jnp.where(take_hi, hi, lo)


def _dprep_kernel(pq_ref, pi_ref, qg_ref, kg_ref, qa_ref, qb_ref, k_ref, v_ref,
                  iq_ref, ik_ref, iw_ref):
    tm = pq_ref.shape[0]
    d = D_BRANCH
    row = _iota((LANES, LANES), 0)
    col = _iota((LANES, LANES), 1)
    bd_avg = jnp.where((row < HEAD_DIM) == (col < HEAD_DIM), 1.0 / HEAD_DIM, 0.0).astype(F32)
    lane_h0 = _iota((tm, LANES), 1) < HEAD_DIM
    q_scale = (HEAD_DIM ** -0.5) * LOG2E
    for p in range(N_PAIRS):
        sl = slice(p * LANES, (p + 1) * LANES)
        q = pq_ref[:, p * LANES:(p + 1) * LANES]
        k = pq_ref[:, d + p * LANES:d + (p + 1) * LANES]
        qn = q * lax.rsqrt(_hdot(q * q, bd_avg) + NORM_EPS) * qg_ref[:, sl]
        kn = k * lax.rsqrt(_hdot(k * k, bd_avg) + NORM_EPS) * kg_ref[:, sl]
        qs = (qn * q_scale).astype(BF16)
        qa_ref[:, sl] = jnp.where(lane_h0, qs, jnp.zeros_like(qs))
        qb_ref[:, sl] = jnp.where(lane_h0, jnp.zeros_like(qs), qs)
        k_ref[:, sl] = kn.astype(BF16)
    v_ref[...] = pq_ref[:, 2 * d:3 * d].astype(BF16)
    lane4 = _iota((tm, 4 * HEAD_DIM), 1) // HEAD_DIM
    q_hi = (lane4 % 2) == 0
    k_hi = lane4 < 2
    off = N_HEADS * 4 * HEAD_DIM
    iw = pi_ref[:, off + 4 * HEAD_DIM:off + 4 * HEAD_DIM + LANES] * (
        (N_HEADS ** -0.5) * (HEAD_DIM ** -0.5))
    for h in range(N_HEADS):
        sl = slice(h * 4 * HEAD_DIM, (h + 1) * 4 * HEAD_DIM)
        iq_ref[h] = _split_hi_lo(pi_ref[:, sl], q_hi)
        iw_ref[h] = jnp.broadcast_to(iw[:, h:h + 1], (tm, LANES))
    ik_ref[...] = _split_hi_lo(pi_ref[:, off:off + 4 * HEAD_DIM], k_hi)


def _dprep(pq, pi, qg, kg, tm):
    s = pq.shape[0]
    row = lambda n: pl.BlockSpec((tm, n), lambda i: (i, 0))
    hrow = lambda n: pl.BlockSpec((N_HEADS, tm, n), lambda i: (0, i, 0))
    vec = pl.BlockSpec((1, D_BRANCH), lambda i: (0, 0))
    kq = 4 * HEAD_DIM
    return pl.pallas_call(
        _dprep_kernel,
        grid=(s // tm,),
        in_specs=[row(pq.shape[1]), row(pi.shape[1]), vec, vec],
        out_specs=[row(D_BRANCH)] * 4 + [hrow(kq), row(kq), hrow(LANES)],
        out_shape=[jax.ShapeDtypeStruct((s, D_BRANCH), BF16)] * 4
                  + [jax.ShapeDtypeStruct((N_HEADS, s, kq), BF16),
                     jax.ShapeDtypeStruct((s, kq), BF16),
                     jax.ShapeDtypeStruct((N_HEADS, s, LANES), F32)],
        compiler_params=pltpu.CompilerParams(dimension_semantics=("arbitrary",),
                                             vmem_limit_bytes=VMEM_LIMIT),
        name="dprep",
    )(pq, pi, qg, kg)


def _block_tables(s):
    qb_l, kb_l, nkb_l = [], [], []
    for qb in range(s // Q_BLOCK):
        nkb = -(-(Q_BLOCK * (qb + 1)) // KEY_BLOCK)
        for kb in range(nkb):
            qb_l.append(qb)
            kb_l.append(kb)
            nkb_l.append(nkb)
    return (np.asarray(qb_l, np.int32), np.asarray(kb_l, np.int32), np.asarray(nkb_l, np.int32))


def _float_to_ckey(x):
    b = pltpu.bitcast(x, I32)
    k = b ^ ((b >> 31) & 0x7FFFFFFF)
    return jnp.where(k >= MIN_NORMAL_BITS, k - (MIN_NORMAL_BITS - 1),
                     jnp.where(k < -MIN_NORMAL_BITS, k + MIN_NORMAL_BITS, 0))


def _ckey_to_float(c):
    k = jnp.where(c > 0, c + (MIN_NORMAL_BITS - 1), jnp.where(c < 0, c - MIN_NORMAL_BITS, 0))
    return pltpu.bitcast(k ^ ((k >> 31) & 0x7FFFFFFF), F32)


def _select_kernel(qb_tab, kb_tab, nkb_tab, iq_ref, iw_ref, ik_ref, mask_ref, sc_ref, cnt_ref,
                   cand_ref, *, topk, seq):
    step = pl.program_id(0)
    qb = qb_tab[step]
    kb = kb_tab[step]
    nkb = nkb_tab[step]
    rows = Q_BLOCK

    lane_pos = _iota((ROW_SUB, KEY_SUB), 1)
    for rh in range(rows // ROW_SUB):
        rs = slice(rh * ROW_SUB, (rh + 1) * ROW_SUB)
        q_chunk = (qb * Q_BLOCK + rh * ROW_SUB + _iota((ROW_SUB, KEY_SUB), 0)) // CHUNK
        iq_all = iq_ref[:, rs, :].reshape(N_HEADS * ROW_SUB, 4 * HEAD_DIM)
        for kc in range(KEY_BLOCK // KEY_SUB):
            sc = _dot_nt(iq_all, ik_ref[kc * KEY_SUB:(kc + 1) * KEY_SUB, :])
            acc = jnp.zeros((ROW_SUB, KEY_SUB), F32)
            for h in range(N_HEADS):
                w = iw_ref[h, rs, :]
                acc = acc + jnp.maximum(sc[h * ROW_SUB:(h + 1) * ROW_SUB], 0.0) * jnp.concatenate(
                    [w] * (KEY_SUB // LANES), axis=1)
            base = kb * KEY_BLOCK + kc * KEY_SUB
            adm = ((base + lane_pos) // CHUNK) <= q_chunk
            sc_ref[rs, pl.ds(pl.multiple_of(base, KEY_SUB), KEY_SUB)] = jnp.where(adm, acc, -jnp.inf)

    @pl.when(kb == nkb - 1)
    def _():
        n_cnt = nkb * (KEY_BLOCK // COUNT_BLOCK)
        n_lb = COUNT_BLOCK // LANES

        def sweep(fn, init):
            outs = []
            for rh in range(rows // ROW_SUB):
                rs = slice(rh * ROW_SUB, (rh + 1) * ROW_SUB)

                def body(j, carry, rs=rs):
                    c0 = pl.multiple_of(j * COUNT_BLOCK, COUNT_BLOCK)
                    blk = sc_ref[rs, pl.ds(c0, COUNT_BLOCK)]
                    for i in range(n_lb):
                        carry = fn(carry, blk[:, i * LANES:(i + 1) * LANES], c0 + i * LANES, i % 2, rs)
                    return carry
                outs.append(lax.fori_loop(0, n_cnt, body, init))
            return jax.tree.map(lambda *xs: jnp.concatenate(xs, axis=0), *outs)

        def lane_reduce(x, red):
            return red(jnp.transpose(x), axis=0, keepdims=True)

        def row_bcast(x):
            return jnp.transpose(jnp.broadcast_to(x, (LANES, rows)))

        def count(pred):
            part = sweep(lambda part, blk, c0, par, rs: part + jnp.where(pred(blk, c0, rs), 1.0, 0.0),
                         jnp.zeros((ROW_SUB, LANES), F32))
            return lane_reduce(part, jnp.sum)

        def count_ge(cand):
            cand_ref[...] = row_bcast(cand)
            return count(lambda blk, c0, rs: blk >= cand_ref[rs, :])

        neg = jnp.full((ROW_SUB, LANES), -jnp.inf, F32)
        g0, g1 = sweep(lambda g, blk, c0, par, rs: ((jnp.maximum(g[0], blk), g[1]) if par == 0
                                                     else (g[0], jnp.maximum(g[1], blk))), (neg, neg))
        row_max = lane_reduce(jnp.maximum(g0, g1), jnp.max)
        lo = lane_reduce(jnp.minimum(g0, g1), jnp.min)
        lo = _ckey_to_float(_float_to_ckey(jnp.maximum(lo - jnp.abs(lo) * BRACKET_SLACK, -FLT_MAX)))
        hi = _ckey_to_float(_float_to_ckey(row_max) + 1)
        cnt = count_ge(lo)
        short = jnp.where(cnt < topk, 1, 0)

        @pl.when(jnp.max(short) > 0)
        def _():
            cnt_ref[...] = count_ge(jnp.full((1, rows), -FLT_MAX, F32))

        @pl.when(jnp.max(short) == 0)
        def _():
            cnt_ref[...] = cnt

        lo = jnp.where(short > 0, -FLT_MAX, lo)
        cnt = jnp.where(short > 0, cnt_ref[...], cnt)
        act = jnp.where(cnt > topk, 1, 0)

        def cond(st):
            return st[5] > 0

        def body(st):
            lo, hi, cnt, act, rnd, _ = st
            lk, hk = _float_to_ckey(lo), _float_to_ckey(hi)
            mid_k = _ckey_to_float((lk >> 1) + (hk >> 1) + (lk & hk & 1))
            mid_v = _ckey_to_float(_float_to_ckey(0.5 * lo + 0.5 * hi))
            use_v = jnp.where(rnd < VALUE_ROUNDS, 1, 0) * jnp.where(mid_v > lo, 1, 0) * jnp.where(
                mid_v < hi, 1, 0)
            mid = jnp.where(use_v > 0, mid_v, mid_k)
            inside = act * jnp.where(mid > lo, 1, 0) * jnp.where(mid < hi, 1, 0)
            c = count_ge(mid)
            up = inside * jnp.where(c >= topk, 1, 0)
            dn = inside - up
            lo = jnp.where(up > 0, mid, lo)
            cnt = jnp.where(up > 0, c, cnt)
            hi = jnp.where(dn > 0, mid, hi)
            act = inside * jnp.where(cnt > topk, 1, 0)
            return lo, hi, cnt, act, rnd + 1, jnp.max(act)

        lo, hi, cnt, act, _, _ = lax.while_loop(
            cond, body, (lo, hi, cnt, act, jnp.int32(0), jnp.max(act)))
        cand_ref[...] = row_bcast(lo)
        excess = jnp.max(cnt) > topk

        def write_mask(sel):
            for rh in range(rows // ROW_SUB):
                rs = slice(rh * ROW_SUB, (rh + 1) * ROW_SUB)

                def body(j, carry, rs=rs):
                    c0 = pl.multiple_of(j * COUNT_BLOCK, COUNT_BLOCK)
                    blk = sc_ref[rs, pl.ds(c0, COUNT_BLOCK)]
                    out = jnp.concatenate(
                        [jnp.where(sel(blk[:, i * LANES:(i + 1) * LANES], c0 + i * LANES, rs), 1.0, 0.0)
                         for i in range(n_lb)], axis=1)
                    mask_ref[rs, pl.ds(c0, COUNT_BLOCK)] = out.astype(jnp.int8)
                    return carry
                lax.fori_loop(0, n_cnt, body, 0)

        @pl.when(jnp.logical_not(excess))
        def _():
            write_mask(lambda blk, c0, rs: blk >= cand_ref[rs, :])

        @pl.when(excess)
        def _():
            n_gt = count(lambda blk, c0, rs: blk > cand_ref[rs, :])
            need = topk - n_gt
            lane = _iota((ROW_SUB, LANES), 1)

            def count_tie_upto(cut):
                cutb = row_bcast(cut)
                return count(lambda blk, c0, rs: (blk == cand_ref[rs, :]) & ((c0 + lane) <= cutb[rs]))

            n_bits = int(seq - 1).bit_length()

            def pos_body(i, cut):
                cand = cut - jnp.left_shift(jnp.int32(1), n_bits - 1 - i)
                ok = count_tie_upto(cand) >= need
                return jnp.where(ok, cand, cut)

            cut0 = jnp.full((1, rows), (1 << n_bits) - 1, I32)
            cut = lax.fori_loop(0, n_bits, pos_body, cut0)
            cutb = row_bcast(cut)
            write_mask(lambda blk, c0, rs: (blk > cand_ref[rs, :])
                       | ((blk == cand_ref[rs, :]) & ((c0 + lane) <= cutb[rs])))

        def zero_body(j, carry):
            c0 = pl.multiple_of(j * COUNT_BLOCK, COUNT_BLOCK)
            mask_ref[:, pl.ds(c0, COUNT_BLOCK)] = jnp.zeros((rows, COUNT_BLOCK), jnp.int8)
            return carry
        lax.fori_loop(n_cnt, seq // COUNT_BLOCK, zero_body, 0)


def _select(iq, iw, ik, topk):
    s = ik.shape[0]
    qb_t, kb_t, nkb_t = _block_tables(s)
    grid_spec = pltpu.PrefetchScalarGridSpec(
        num_scalar_prefetch=3,
        grid=(len(qb_t),),
        in_specs=[pl.BlockSpec((N_HEADS, Q_BLOCK, iq.shape[2]), lambda i, qb, kb, nk: (0, qb[i], 0)),
                  pl.BlockSpec((N_HEADS, Q_BLOCK, LANES), lambda i, qb, kb, nk: (0, qb[i], 0)),
                  pl.BlockSpec((KEY_BLOCK, ik.shape[1]), lambda i, qb, kb, nk: (kb[i], 0))],
        out_specs=pl.BlockSpec((Q_BLOCK, s), lambda i, qb, kb, nk: (qb[i], 0)),
        scratch_shapes=[pltpu.VMEM((Q_BLOCK, s + LANES), F32),
                        pltpu.VMEM((1, Q_BLOCK), F32),
                        pltpu.VMEM((Q_BLOCK, LANES), F32)],
    )
    return pl.pallas_call(
        functools.partial(_select_kernel, topk=topk, seq=s),
        grid_spec=grid_spec,
        out_shape=jax.ShapeDtypeStruct((s, s), jnp.int8),
        compiler_params=pltpu.CompilerParams(dimension_semantics=("arbitrary",),
                                             vmem_limit_bytes=VMEM_LIMIT),
        name="select",
    )(jnp.asarray(qb_t), jnp.asarray(kb_t), jnp.asarray(nkb_t), iq, iw, ik)


def _attn_kernel(qb_tab, kb_tab, nkb_tab, qa_ref, qb_ref, k_ref, v_ref, mask_ref, o_ref,
                 m_ref, acc_ref, bias_ref, s_ref):
    step = pl.program_id(0)
    kb = kb_tab[step]
    nkb = nkb_tab[step]
    n_sub = KEY_BLOCK // KEY_SUB

    @pl.when(kb == 0)
    def _():
        m_ref[...] = jnp.full(m_ref.shape, NEG_BIG, F32)
        acc_ref[...] = jnp.zeros_like(acc_ref)

    for kc in range(n_sub):
        ks = slice(kc * KEY_SUB, (kc + 1) * KEY_SUB)
        bias_ref[:, ks] = (mask_ref[:, ks].astype(F32) - 1.0) * (-NEG_BIG)

    ones = jnp.ones((KEY_SUB, LANES), BF16)
    q_refs = (qa_ref, qb_ref)
    lanes_of = lambda h: slice((h // 2) * LANES, (h // 2 + 1) * LANES)
    m_old, m_new = [], []
    for h in range(N_HEADS):
        sl = lanes_of(h)
        q = q_refs[h % 2][:, sl]
        bm = None
        for kc in range(n_sub):
            ks = slice(kc * KEY_SUB, (kc + 1) * KEY_SUB)
            s = _dot_nt(q, k_ref[ks, sl]) + bias_ref[:, ks]
            s_ref[h, :, ks] = s
            mx = jnp.max(s, axis=1, keepdims=True)
            bm = mx if bm is None else jnp.maximum(bm, mx)
        m_old.append(m_ref[h])
        m_new.append(jnp.maximum(m_old[h], bm))
    for h in range(N_HEADS):
        sl = lanes_of(h)
        m2 = jnp.concatenate([m_new[h]] * (KEY_SUB // LANES), axis=1)
        pv = None
        for kc in range(n_sub):
            ks = slice(kc * KEY_SUB, (kc + 1) * KEY_SUB)
            prob = jnp.exp2(s_ref[h, :, ks] - m2).astype(BF16)
            v_aug = jnp.concatenate([v_ref[ks, sl], ones], axis=1)
            d = jnp.dot(prob, v_aug, preferred_element_type=F32)
            pv = d if pv is None else pv + d
        alpha = jnp.exp2(m_old[h] - m_new[h])
        acc_ref[h] = acc_ref[h] * jnp.concatenate([alpha, alpha], axis=1) + pv
        m_ref[h] = m_new[h]

    @pl.when(kb == nkb - 1)
    def _():
        lane_h0 = _iota((Q_BLOCK, LANES), 1) < HEAD_DIM
        for p in range(N_PAIRS):
            sl = slice(p * LANES, (p + 1) * LANES)
            a0 = acc_ref[2 * p]
            a1 = acc_ref[2 * p + 1]
            o_ref[:, sl] = jnp.where(lane_h0, a0[:, 0:LANES] / a0[:, LANES:2 * LANES],
                                     a1[:, 0:LANES] / a1[:, LANES:2 * LANES])


def _attn(qa, qb, k, v, mask):
    s = k.shape[0]
    qb_t, kb_t, nkb_t = _block_tables(s)
    qspec = pl.BlockSpec((Q_BLOCK, D_BRANCH), lambda i, qb, kb, nk: (qb[i], 0))
    kspec = pl.BlockSpec((KEY_BLOCK, D_BRANCH), lambda i, qb, kb, nk: (kb[i], 0))
    grid_spec = pltpu.PrefetchScalarGridSpec(
        num_scalar_prefetch=3,
        grid=(len(qb_t),),
        in_specs=[qspec, qspec, kspec, kspec,
                  pl.BlockSpec((Q_BLOCK, KEY_BLOCK), lambda i, qb, kb, nk: (qb[i], kb[i]))],
        out_specs=qspec,
        scratch_shapes=[pltpu.VMEM((N_HEADS, Q_BLOCK, LANES), F32),
                        pltpu.VMEM((N_HEADS, Q_BLOCK, 2 * LANES), F32),
                        pltpu.VMEM((Q_BLOCK, KEY_BLOCK), F32),
                        pltpu.VMEM((N_HEADS, Q_BLOCK, KEY_BLOCK), F32)],
    )
    return pl.pallas_call(
        _attn_kernel,
        grid_spec=grid_spec,
        out_shape=jax.ShapeDtypeStruct((s, D_BRANCH), F32),
        compiler_params=pltpu.CompilerParams(dimension_semantics=("arbitrary",),
                                             vmem_limit_bytes=VMEM_LIMIT),
        name="attn",
    )(jnp.asarray(qb_t), jnp.asarray(kb_t), jnp.asarray(nkb_t), qa, qb, k, v, mask)


def _outproj_kernel(x_ref, za_ref, yb_ref, gb_ref, pg_ref, gate_ref, wa_ref, wb_ref, wo_ref,
                    o_ref):
    d = x_ref.shape[1]
    gb = gb_ref[...]
    zb = yb_ref[...] * (gb * _sigmoid(gb))
    ya = jnp.dot(za_ref[...].astype(BF16), wa_ref[...], preferred_element_type=F32)
    yb = jnp.dot(zb.astype(BF16), wb_ref[...], preferred_element_type=F32)
    merged = _sigmoid(pg_ref[:, 0:d]) * ya + _sigmoid(pg_ref[:, d:2 * d]) * yb
    out = jnp.dot(merged.astype(BF16), wo_ref[...], preferred_element_type=F32)
    o_ref[...] = x_ref[...] + gate_ref[...] * out


def _outproj(x2, za, yb, pq, pg, gate, wa, wb, wo, tm):
    s, d = x2.shape
    row = lambda n: pl.BlockSpec((tm, n), lambda i: (i, 0))
    wsp = lambda w: pl.BlockSpec(w.shape, lambda i: (0, 0))
    gb_col = 3 * D_BRANCH // D_BRANCH
    return pl.pallas_call(
        _outproj_kernel,
        grid=(s // tm,),
        in_specs=[row(d), row(D_BRANCH), row(D_BRANCH),
                  pl.BlockSpec((tm, D_BRANCH), lambda i: (i, gb_col)),
                  row(2 * d), pl.BlockSpec((1, d), lambda i: (0, 0)),
                  wsp(wa), wsp(wb), wsp(wo)],
        out_specs=row(d),
        out_shape=jax.ShapeDtypeStruct((s, d), F32),
        compiler_params=pltpu.CompilerParams(dimension_semantics=("arbitrary",),
                                             vmem_limit_bytes=VMEM_LIMIT),
        name="outproj",
    )(x2, za, yb, pq, pg, gate, wa, wb, wo)


def _pad_cols(w, n):
    return jnp.pad(w, ((0, 0), (0, n - w.shape[1])))


def _layer(x2, c, norm_w, w_ada, b_ada, w_in, mu, w0, w_up, a0, a_up, k_k, k_a, r_k, gn_w,
           gn_b, q_gain, k_gain, w_a_out, w_b_out, w_o):
    s, d = x2.shape
    db = D_BRANCH
    assert s % KEY_BLOCK == 0 and d % LANES == 0
    topk = min(TOPK_MAX, s // 4)

    mod = _ada(jnp.broadcast_to(c, (8, d)), w_ada, b_ada[None, :])[0:1]
    shift, scale, gate = mod[:, 0:d], mod[:, d:2 * d], mod[:, 2 * d:3 * d]

    n_rwkv = 4 * db + 2 * LORA
    def rwkv_cols(w):
        return jnp.concatenate(
            [w[:, 0:3 * db], w[:, 3 * db + 2 * LORA:n_rwkv],
             _pad_cols(w[:, 3 * db:3 * db + LORA], LANES),
             _pad_cols(w[:, 3 * db + LORA:3 * db + 2 * LORA], LANES)], axis=1)
    wa = rwkv_cols(w_in[:, 0:n_rwkv]).astype(BF16)
    mu_a = rwkv_cols(mu[None, :])
    o = n_rwkv
    wq = w_in[:, o:o + 4 * db].astype(BF16)
    o += 4 * db
    w_iq = w_in[:, o:o + N_HEADS * HEAD_DIM].reshape(d, N_HEADS, 1, HEAD_DIM)
    w_iq = jnp.broadcast_to(w_iq, (d, N_HEADS, 4, HEAD_DIM)).reshape(d, N_HEADS * 4 * HEAD_DIM)
    o += N_HEADS * HEAD_DIM
    w_ik = jnp.tile(w_in[:, o:o + HEAD_DIM], (1, 4))
    o += HEAD_DIM
    w_iw = _pad_cols(w_in[:, o:o + N_HEADS], LANES)
    o += N_HEADS
    wi = jnp.concatenate([w_iq, w_ik, w_iw], axis=1).astype(BF16)
    wg = w_in[:, o:o + 2 * d].astype(BF16)

    pa, pq, pi, pg = _inproj(x2, norm_w[None, :], scale, shift, wa, wq, wi, wg, tm=256)

    pad_rows = lambda w: jnp.pad(w, ((0, LANES - w.shape[0]), (0, 0)))
    za = _rwkv(pa, mu_a, w0[None, :], a0[None, :], k_k[None, :], k_a[None, :],
               r_k.reshape(1, db), gn_w[None, :], gn_b[None, :], pad_rows(w_up), pad_rows(a_up),
               tm=512)

    qg = jnp.tile(q_gain, N_HEADS)[None, :]
    kg = jnp.tile(k_gain, N_HEADS)[None, :]
    qa, qb, kn, vb, iq, ik, iw = _dprep(pq, pi, qg, kg, tm=512)
    mask = _select(iq, iw, ik, topk)
    yb = _attn(qa, qb, kn, vb, mask)

    return _outproj(x2, za, yb, pq, pg, gate, w_a_out.astype(BF16), w_b_out.astype(BF16),
                    w_o.astype(BF16), tm=512)


def kernel(x, c, norm_w, w_ada, b_ada, w_in, mu, w0, w_up, a0, a_up, k_k, k_a, r_k, gn_w, gn_b,
           q_gain, k_gain, w_a_out, w_b_out, w_o):
    b, s, d = x.shape
    outs = []
    for bi in range(b):
        xb = x[bi]
        for l in range(norm_w.shape[0]):
            xb = _layer(xb, c[bi:bi + 1], norm_w[l], w_ada[l], b_ada[l], w_in[l], mu[l], w0[l],
                        w_up[l], a0[l], a_up[l], k_k[l], k_a[l], r_k[l], gn_w[l], gn_b[l],
                        q_gain[l], k_gain[l], w_a_out[l], w_b_out[l], w_o[l])
        outs.append(xb)
    return jnp.stack(outs, axis=0)
```

```python
import functools

import numpy as np
import jax
import jax.numpy as jnp
from jax import lax
from jax.experimental import pallas as pl
from jax.experimental.pallas import tpu as pltpu

F32 = jnp.float32
BF16 = jnp.bfloat16
I32 = jnp.int32
HIGHEST = lax.Precision.HIGHEST

CHUNK = 64
HEAD_DIM = 64
N_HEADS = 8
D_BRANCH = N_HEADS * HEAD_DIM
LORA = 64
DECAY_SCALE = 0.606531
GN_EPS = 64e-5
NORM_EPS = 1e-6
TOPK_MAX = 256
Q_BLOCK = 256

LANES = 128
N_PAIRS = D_BRANCH // LANES
VMEM_LIMIT = 56 * 1024 * 1024

KEY_BLOCK = 1024
KEY_SUB = 256
ROW_SUB = 128
COUNT_BLOCK = 512
VALUE_ROUNDS = 20
BRACKET_SLACK = 2.0 ** -10
FLT_MAX = 3.4028234663852886e38
MIN_NORMAL_BITS = 0x00800000
MIN_NORMAL = 1.1754943508222875e-38
NEG_BIG = -1e30
LOG2E = 1.4426950408889634

NA = 4 * D_BRANCH + 2 * LANES
NI = N_HEADS * 4 * HEAD_DIM + 4 * HEAD_DIM + LANES


def _sigmoid(x):
    return 1.0 / (1.0 + jnp.exp(-x))


def _hdot(a, b):
    return jnp.dot(a, b, precision=HIGHEST, preferred_element_type=F32)


def _split(x):
    hi = x.astype(BF16)
    lo = (x - hi.astype(F32)).astype(BF16)
    return hi, lo


def _dot(a, b, dims):
    return lax.dot_general(a, b, (dims, ((), ())), preferred_element_type=F32)


_NN = ((1,), (0,))
_NT = ((1,), (1,))
_TN = ((0,), (0,))


def _mm(sa, sb, dims=_NN):
    (ah, al), (bh, bl) = sa, sb
    ca, cb = dims[0][0], dims[1][0]
    return (_dot(jnp.concatenate([ah, al], axis=ca), jnp.concatenate([bh, bh], axis=cb), dims)
            + _dot(ah, bl, dims))


def _mm_exact_rhs(sa, b, dims=_NN):
    ah, al = sa
    ca, cb = dims[0][0], dims[1][0]
    return _dot(jnp.concatenate([ah, al], axis=ca), jnp.concatenate([b, b], axis=cb), dims)


def _mm_exact_lhs(a, sb, dims=_NN):
    bh, bl = sb
    ca, cb = dims[0][0], dims[1][0]
    return _dot(jnp.concatenate([a, a], axis=ca), jnp.concatenate([bh, bl], axis=cb), dims)


def _dot_nt(a, b):
    return lax.dot_general(a, b, (((1,), (1,)), ((), ())), preferred_element_type=F32)


def _iota(shape, dim):
    return lax.broadcasted_iota(I32, shape, dim)


def _ada_kernel(c_ref, w_ref, b_ref, o_ref):
    c = c_ref[...]
    s = c * _sigmoid(c)
    o_ref[...] = _hdot(s, w_ref[...]) + b_ref[...]


def _ada(c8, w_ada, b_ada):
    d = c8.shape[1]
    n = w_ada.shape[1]
    return pl.pallas_call(
        _ada_kernel,
        grid=(n // d,),
        in_specs=[pl.BlockSpec((8, d), lambda j: (0, 0)),
                  pl.BlockSpec((d, d), lambda j: (0, j)),
                  pl.BlockSpec((1, d), lambda j: (0, j))],
        out_specs=pl.BlockSpec((8, d), lambda j: (0, j)),
        out_shape=jax.ShapeDtypeStruct((8, n), F32),
        name="ada",
    )(c8, w_ada, b_ada)


def _inproj_kernel(x_ref, nw_ref, sc_ref, sh_ref, wa_ref, wq_ref, wi_ref, wg_ref,
                   pa_ref, pq_ref, pi_ref, pg_ref):
    x = x_ref[...]
    ms = jnp.mean(x * x, axis=-1, keepdims=True)
    h = x * lax.rsqrt(ms + NORM_EPS) * nw_ref[...]
    h = h * (1.0 + sc_ref[...]) + sh_ref[...]
    hb = h.astype(BF16)
    pa_ref[...] = jnp.dot(hb, wa_ref[...], preferred_element_type=F32)
    pq_ref[...] = jnp.dot(hb, wq_ref[...], preferred_element_type=F32)
    pi_ref[...] = jnp.dot(hb, wi_ref[...], preferred_element_type=F32)
    pg_ref[...] = jnp.dot(hb, wg_ref[...], preferred_element_type=F32)


def _inproj(x2, nw, scale, shift, wa, wq, wi, wg, tm):
    s, d = x2.shape
    row = lambda n: pl.BlockSpec((tm, n), lambda i: (i, 0))
    vec = pl.BlockSpec((1, d), lambda i: (0, 0))
    wsp = lambda w: pl.BlockSpec(w.shape, lambda i: (0, 0))
    return pl.pallas_call(
        _inproj_kernel,
        grid=(s // tm,),
        in_specs=[row(d), vec, vec, vec, wsp(wa), wsp(wq), wsp(wi), wsp(wg)],
        out_specs=[row(wa.shape[1]), row(wq.shape[1]), row(wi.shape[1]), row(wg.shape[1])],
        out_shape=[jax.ShapeDtypeStruct((s, w.shape[1]), F32) for w in (wa, wq, wi, wg)],
        compiler_params=pltpu.CompilerParams(dimension_semantics=("arbitrary",),
                                             vmem_limit_bytes=VMEM_LIMIT),
        name="inproj",
    )(x2, nw, scale, shift, wa, wq, wi, wg)


def _rwkv_kernel(pa_ref, mu_ref, w0_ref, a0_ref, kk_ref, ka_ref, rk_ref, gnw_ref, gnb_ref,
                 wup_ref, aup_ref, o_ref, s_ref, carry_ref, *, n_chunks):
    @pl.when(pl.program_id(0) == 0)
    def _():
        s_ref[...] = jnp.zeros_like(s_ref)
        carry_ref[...] = jnp.zeros_like(carry_ref)

    row = _iota((LANES, LANES), 0)
    col = _iota((LANES, LANES), 1)
    strict = row > col
    incl = row >= col
    same_head = (row < HEAD_DIM) == (col < HEAD_DIM)
    bd_ones = jnp.where(same_head, 1.0, 0.0).astype(BF16)
    eye = jnp.where(row == col, 1.0, 0.0).astype(F32)
    tri = jnp.where(_iota((CHUNK, CHUNK), 0) >= _iota((CHUNK, CHUNK), 1), 1.0, 0.0).astype(BF16)
    lane_h0 = _iota((CHUNK, LANES), 1) < HEAD_DIM
    first_row = _iota((CHUNK, NA), 0) == 0
    wup = _split(wup_ref[...])
    aup = _split(aup_ref[...])

    def stack(a):
        return jnp.concatenate([jnp.where(lane_h0, a, 0.0), jnp.where(lane_h0, 0.0, a)], axis=0)

    def chunk_body(c, carry):
        r0 = pl.multiple_of(c * CHUNK, CHUNK)
        x = pa_ref[pl.ds(r0, CHUNK), :]
        prev = pltpu.roll(x, 1, 0)
        prev = jnp.where(first_row, carry_ref[7:8, :], prev)
        carry_ref[...] = x[CHUNK - 8:CHUNK, :]
        xs = x + mu_ref[...] * (prev - x)
        d = D_BRANCH
        r = xs[:, 0:d]
        k = xs[:, d:2 * d]
        v = xs[:, 2 * d:3 * d]
        g = xs[:, 3 * d:4 * d]
        wd = xs[:, 4 * d:4 * d + LANES]
        ad = xs[:, 4 * d + LANES:4 * d + 2 * LANES]
        lw = -DECAY_SCALE * _sigmoid(w0_ref[...] + _mm(_split(jnp.tanh(wd)), wup))
        a = _sigmoid(a0_ref[...] + _mm(_split(ad), aup))
        kkx = k * kk_ref[...]
        k2 = k * (1.0 + (a - 1.0) * ka_ref[...])
        bon = r * k2 * rk_ref[...]

        sls = [slice(p * LANES, (p + 1) * LANES) for p in range(N_PAIRS)]
        pre = []
        for sl in sls:
            rp, kp, lwp, kkxp = r[:, sl], k2[:, sl], lw[:, sl], kkx[:, sl]
            ss = _mm_exact_rhs(_split(kkxp * kkxp), bd_ones)
            kkp = kkxp / jnp.maximum(jnp.sqrt(ss), 1e-12)
            alpha = -kkp
            beta = kkp * a[:, sl]
            cl = _mm_exact_lhs(tri, _split(lwp))
            clp = cl - lwp
            cm = cl[CHUNK // 2 - 1:CHUNK // 2, :]
            cend = cl[CHUNK - 1:CHUNK, :]
            e_k = jnp.exp(cm - cl)
            e_h = jnp.exp(cend - cl)
            lhs = _split(jnp.concatenate([stack(alpha * jnp.exp(clp - cm)),
                                          stack(rp * jnp.exp(cl - cm))], axis=0))
            rhs = _split(jnp.concatenate([stack(beta * e_k), stack(kp * e_k)], axis=0))
            pre.append(dict(
                gram=_mm(lhs, rhs, _NT),
                a_e=_split(stack(alpha * jnp.exp(clp))),
                r_e=_split(stack(rp * jnp.exp(cl))),
                bk=_split(jnp.concatenate([stack(beta * e_h), stack(kp * e_h)], axis=0)),
                v_st=_split(stack(v[:, sl])),
                g_end=jnp.exp(cend)))
        n_ab = [jnp.where(strict, q["gram"][0:LANES, 0:LANES], 0.0) for q in pre]
        tinv = [eye + n for n in n_ab]
        npow = n_ab
        for _ in range(5):
            sps = [_split(n) for n in npow]
            npow = [_mm(sp, sp) for sp in sps]
            tinv = [t + _mm(_split(t), _split(n)) for t, n in zip(tinv, npow)]
        s0 = [s_ref[p] for p in range(N_PAIRS)]
        s0s = [_split(x0) for x0 in s0]
        u_rhs = [_mm(q["a_e"], ss0, _NT)
                 + _mm(_split(jnp.where(strict, q["gram"][0:LANES, LANES:2 * LANES], 0.0)), q["v_st"])
                 for q, ss0 in zip(pre, s0s)]
        u_sp = [_split(_mm(_split(t), _split(u))) for t, u in zip(tinv, u_rhs)]
        for p, (q, us) in enumerate(zip(pre, u_sp)):
            uv = (jnp.concatenate([us[0], q["v_st"][0]], axis=0),
                  jnp.concatenate([us[1], q["v_st"][1]], axis=0))
            s_ref[p] = s0[p] * q["g_end"] + _mm(uv, q["bk"], _TN)
        ys = []
        for q, ss0, us in zip(pre, s0s, u_sp):
            m_rb = jnp.where(incl, q["gram"][LANES:2 * LANES, 0:LANES], 0.0)
            m_rk = jnp.where(incl, q["gram"][LANES:2 * LANES, LANES:2 * LANES], 0.0)
            y_st = _mm(q["r_e"], ss0, _NT) + _mm(_split(m_rb), us) + _mm(_split(m_rk), q["v_st"])
            ys.append(y_st[0:CHUNK] + y_st[CHUNK:2 * CHUNK])
        bonus = [_mm_exact_rhs(_split(bon[:, sl]), bd_ones) * v[:, sl] for sl in sls]
        dlt = [y - _mm_exact_rhs(_split(y), bd_ones) * (1.0 / HEAD_DIM) for y in ys]
        var = [_mm_exact_rhs(_split(dl * dl), bd_ones) * (1.0 / HEAD_DIM) for dl in dlt]
        for sl, dl, vr, bo in zip(sls, dlt, var, bonus):
            yn = dl * lax.rsqrt(vr + GN_EPS) * gnw_ref[:, sl] + gnb_ref[:, sl]
            gp = g[:, sl]
            o_ref[pl.ds(r0, CHUNK), sl] = (yn + bo) * (gp * _sigmoid(gp))
        return carry

    lax.fori_loop(0, n_chunks, chunk_body, 0)


def _rwkv(pa, mu, w0, a0, k_k, k_a, r_k, gn_w, gn_b, w_up, a_up, tm):
    s = pa.shape[0]
    vec = lambda n: pl.BlockSpec((1, n), lambda i: (0, 0))
    lora = pl.BlockSpec((LANES, D_BRANCH), lambda i: (0, 0))
    return pl.pallas_call(
        functools.partial(_rwkv_kernel, n_chunks=tm // CHUNK),
        grid=(s // tm,),
        in_specs=[pl.BlockSpec((tm, NA), lambda i: (i, 0)), vec(NA)] + [vec(D_BRANCH)] * 7
                 + [lora, lora],
        out_specs=pl.BlockSpec((tm, D_BRANCH), lambda i: (i, 0)),
        out_shape=jax.ShapeDtypeStruct((s, D_BRANCH), F32),
        scratch_shapes=[pltpu.VMEM((N_PAIRS, LANES, LANES), F32),
                        pltpu.VMEM((8, NA), F32)],
        compiler_params=pltpu.CompilerParams(dimension_semantics=("arbitrary",),
                                             vmem_limit_bytes=VMEM_LIMIT),
        name="rwkv",
    )(pa, mu, w0, a0, k_k, k_a, r_k, gn_w, gn_b, w_up, a_up)


def _split_hi_lo(x, take_hi):
    hi = x.astype(BF16)
    lo = (x - hi.astype(F32)).astype(BF16)
    return jnp.where(take_hi, hi, lo)


def _dprep_kernel(pq_ref, pi_ref, qg_ref, kg_ref, qa_ref, qb_ref, k_ref, v_ref,
                  iq_ref, ik_ref, iw_ref):
    tm = pq_ref.shape[0]
    d = D_BRANCH
    row = _iota((LANES, LANES), 0)
    col = _iota((LANES, LANES), 1)
    bd_avg = jnp.where((row < HEAD_DIM) == (col < HEAD_DIM), 1.0 / HEAD_DIM, 0.0).astype(F32)
    lane_h0 = _iota((tm, LANES), 1) < HEAD_DIM
    q_scale = (HEAD_DIM ** -0.5) * LOG2E
    for p in range(N_PAIRS):
        sl = slice(p * LANES, (p + 1) * LANES)
        q = pq_ref[:, p * LANES:(p + 1) * LANES]
        k = pq_ref[:, d + p * LANES:d + (p + 1) * LANES]
        qn = q * lax.rsqrt(_hdot(q * q, bd_avg) + NORM_EPS) * qg_ref[:, sl]
        kn = k * lax.rsqrt(_hdot(k * k, bd_avg) + NORM_EPS) * kg_ref[:, sl]
        qs = (qn * q_scale).astype(BF16)
        qa_ref[:, sl] = jnp.where(lane_h0, qs, jnp.zeros_like(qs))
        qb_ref[:, sl] = jnp.where(lane_h0, jnp.zeros_like(qs), qs)
        k_ref[:, sl] = kn.astype(BF16)
    v_ref[...] = pq_ref[:, 2 * d:3 * d].astype(BF16)
    lane4 = _iota((tm, 4 * HEAD_DIM), 1) // HEAD_DIM
    q_hi = (lane4 % 2) == 0
    k_hi = lane4 < 2
    off = N_HEADS * 4 * HEAD_DIM
    iw = pi_ref[:, off + 4 * HEAD_DIM:off + 4 * HEAD_DIM + LANES] * (
        (N_HEADS ** -0.5) * (HEAD_DIM ** -0.5))
    for h in range(N_HEADS):
        sl = slice(h * 4 * HEAD_DIM, (h + 1) * 4 * HEAD_DIM)
        iq_ref[h] = _split_hi_lo(pi_ref[:, sl], q_hi)
        iw_ref[h] = jnp.broadcast_to(iw[:, h:h + 1], (tm, LANES))
    ik_ref[...] = _split_hi_lo(pi_ref[:, off:off + 4 * HEAD_DIM], k_hi)


def _dprep(pq, pi, qg, kg, tm):
    s = pq.shape[0]
    row = lambda n: pl.BlockSpec((tm, n), lambda i: (i, 0))
    hrow = lambda n: pl.BlockSpec((N_HEADS, tm, n), lambda i: (0, i, 0))
    vec = pl.BlockSpec((1, D_BRANCH), lambda i: (0, 0))
    kq = 4 * HEAD_DIM
    return pl.pallas_call(
        _dprep_kernel,
        grid=(s // tm,),
        in_specs=[row(pq.shape[1]), row(pi.shape[1]), vec, vec],
        out_specs=[row(D_BRANCH)] * 4 + [hrow(kq), row(kq), hrow(LANES)],
        out_shape=[jax.ShapeDtypeStruct((s, D_BRANCH), BF16)] * 4
                  + [jax.ShapeDtypeStruct((N_HEADS, s, kq), BF16),
                     jax.ShapeDtypeStruct((s, kq), BF16),
                     jax.ShapeDtypeStruct((N_HEADS, s, LANES), F32)],
        compiler_params=pltpu.CompilerParams(dimension_semantics=("arbitrary",),
                                             vmem_limit_bytes=VMEM_LIMIT),
        name="dprep",
    )(pq, pi, qg, kg)


def _block_tables(s):
    qb_l, kb_l, nkb_l = [], [], []
    for qb in range(s // Q_BLOCK):
        nkb = -(-(Q_BLOCK * (qb + 1)) // KEY_BLOCK)
        for kb in range(nkb):
            qb_l.append(qb)
            kb_l.append(kb)
            nkb_l.append(nkb)
    return (np.asarray(qb_l, np.int32), np.asarray(kb_l, np.int32), np.asarray(nkb_l, np.int32))


def _float_to_ckey(x):
    b = pltpu.bitcast(x, I32)
    k = b ^ ((b >> 31) & 0x7FFFFFFF)
    return jnp.where(k >= MIN_NORMAL_BITS, k - (MIN_NORMAL_BITS - 1),
                     jnp.where(k < -MIN_NORMAL_BITS, k + MIN_NORMAL_BITS, 0))


def _ckey_to_float(c):
    k = jnp.where(c > 0, c + (MIN_NORMAL_BITS - 1), jnp.where(c < 0, c - MIN_NORMAL_BITS, 0))
    return pltpu.bitcast(k ^ ((k >> 31) & 0x7FFFFFFF), F32)


def _select_kernel(qb_tab, kb_tab, nkb_tab, iq_ref, iw_ref, ik_ref, mask_ref, sc_ref, cnt_ref,
                   cand_ref, *, topk, seq):
    step = pl.program_id(0)
    qb = qb_tab[step]
    kb = kb_tab[step]
    nkb = nkb_tab[step]
    rows = Q_BLOCK

    lane_pos = _iota((ROW_SUB, KEY_SUB), 1)
    for rh in range(rows // ROW_SUB):
        rs = slice(rh * ROW_SUB, (rh + 1) * ROW_SUB)
        q_chunk = (qb * Q_BLOCK + rh * ROW_SUB + _iota((ROW_SUB, KEY_SUB), 0)) // CHUNK
        iq_all = iq_ref[:, rs, :].reshape(N_HEADS * ROW_SUB, 4 * HEAD_DIM)
        for kc in range(KEY_BLOCK // KEY_SUB):
            sc = _dot_nt(iq_all, ik_ref[kc * KEY_SUB:(kc + 1) * KEY_SUB, :])
            acc = jnp.zeros((ROW_SUB, KEY_SUB), F32)
            for h in range(N_HEADS):
                w = iw_ref[h, rs, :]
                acc = acc + jnp.maximum(sc[h * ROW_SUB:(h + 1) * ROW_SUB], 0.0) * jnp.concatenate(
                    [w] * (KEY_SUB // LANES), axis=1)
            base = kb * KEY_BLOCK + kc * KEY_SUB
            adm = ((base + lane_pos) // CHUNK) <= q_chunk
            sc_ref[rs, pl.ds(pl.multiple_of(base, KEY_SUB), KEY_SUB)] = jnp.where(adm, acc, -jnp.inf)

    @pl.when(kb == nkb - 1)
    def _():
        n_cnt = nkb * (KEY_BLOCK // COUNT_BLOCK)
        n_lb = COUNT_BLOCK // LANES

        def sweep(fn, init):
            outs = []
            for rh in range(rows // ROW_SUB):
                rs = slice(rh * ROW_SUB, (rh + 1) * ROW_SUB)

                def body(j, carry, rs=rs):
                    c0 = pl.multiple_of(j * COUNT_BLOCK, COUNT_BLOCK)
                    blk = sc_ref[rs, pl.ds(c0, COUNT_BLOCK)]
                    for i in range(n_lb):
                        carry = fn(carry, blk[:, i * LANES:(i + 1) * LANES], c0 + i * LANES, i % 2, rs)
                    return carry
                outs.append(lax.fori_loop(0, n_cnt, body, init))
            return jax.tree.map(lambda *xs: jnp.concatenate(xs, axis=0), *outs)

        def lane_reduce(x, red):
            return red(jnp.transpose(x), axis=0, keepdims=True)

        def row_bcast(x):
            return jnp.transpose(jnp.broadcast_to(x, (LANES, rows)))

        def count(pred):
            part = sweep(lambda part, blk, c0, par, rs: part + jnp.where(pred(blk, c0, rs), 1.0, 0.0),
                         jnp.zeros((ROW_SUB, LANES), F32))
            return lane_reduce(part, jnp.sum)

        def count_ge(cand):
            cand_ref[...] = row_bcast(cand)
            return count(lambda blk, c0, rs: blk >= cand_ref[rs, :])

        neg = jnp.full((ROW_SUB, LANES), -jnp.inf, F32)
        g0, g1 = sweep(lambda g, blk, c0, par, rs: ((jnp.maximum(g[0], blk), g[1]) if par == 0
                                                     else (g[0], jnp.maximum(g[1], blk))), (neg, neg))
        row_max = lane_reduce(jnp.maximum(g0, g1), jnp.max)
        lo = lane_reduce(jnp.minimum(g0, g1), jnp.min)
        lo = _ckey_to_float(_float_to_ckey(jnp.maximum(lo - jnp.abs(lo) * BRACKET_SLACK, -FLT_MAX)))
        hi = _ckey_to_float(_float_to_ckey(row_max) + 1)
        cnt = count_ge(lo)
        short = jnp.where(cnt < topk, 1, 0)

        @pl.when(jnp.max(short) > 0)
        def _():
            cnt_ref[...] = count_ge(jnp.full((1, rows), -FLT_MAX, F32))

        @pl.when(jnp.max(short) == 0)
        def _():
            cnt_ref[...] = cnt

        lo = jnp.where(short > 0, -FLT_MAX, lo)
        cnt = jnp.where(short > 0, cnt_ref[...], cnt)
        act = jnp.where(cnt > topk, 1, 0)

        def cond(st):
            return st[5] > 0

        def body(st):
            lo, hi, cnt, act, rnd, _ = st
            lk, hk = _float_to_ckey(lo), _float_to_ckey(hi)
            mid_k = _ckey_to_float((lk >> 1) + (hk >> 1) + (lk & hk & 1))
            mid_v = _ckey_to_float(_float_to_ckey(0.5 * lo + 0.5 * hi))
            use_v = jnp.where(rnd < VALUE_ROUNDS, 1, 0) * jnp.where(mid_v > lo, 1, 0) * jnp.where(
                mid_v < hi, 1, 0)
            mid = jnp.where(use_v > 0, mid_v, mid_k)
            straddle = jnp.where(lo < 0.0, 1, 0) * jnp.where(hi > 0.0, 1, 0)
            at_zero = jnp.where(lo == 0.0, 1, 0) * jnp.where(hi > MIN_NORMAL, 1, 0)
            mid = jnp.where(straddle > 0, 0.0, jnp.where(at_zero > 0, MIN_NORMAL, mid))
            inside = act * jnp.where(mid > lo, 1, 0) * jnp.where(mid < hi, 1, 0)
            c = count_ge(mid)
            up = inside * jnp.where(c >= topk, 1, 0)
            dn = inside - up
            lo = jnp.where(up > 0, mid, lo)
            cnt = jnp.where(up > 0, c, cnt)
            hi = jnp.where(dn > 0, mid, hi)
            act = inside * jnp.where(cnt > topk, 1, 0)
            return lo, hi, cnt, act, rnd + 1, jnp.max(act)

        lo, hi, cnt, act, _, _ = lax.while_loop(
            cond, body, (lo, hi, cnt, act, jnp.int32(0), jnp.max(act)))
        cand_ref[...] = row_bcast(lo)
        excess = jnp.max(cnt) > topk

        def write_mask(sel):
            for rh in range(rows // ROW_SUB):
                rs = slice(rh * ROW_SUB, (rh + 1) * ROW_SUB)

                def body(j, carry, rs=rs):
                    c0 = pl.multiple_of(j * COUNT_BLOCK, COUNT_BLOCK)
                    blk = sc_ref[rs, pl.ds(c0, COUNT_BLOCK)]
                    out = jnp.concatenate(
                        [jnp.where(sel(blk[:, i * LANES:(i + 1) * LANES], c0 + i * LANES, rs), 1.0, 0.0)
                         for i in range(n_lb)], axis=1)
                    mask_ref[rs, pl.ds(c0, COUNT_BLOCK)] = out.astype(jnp.int8)
                    return carry
                lax.fori_loop(0, n_cnt, body, 0)

        @pl.when(jnp.logical_not(excess))
        def _():
            write_mask(lambda blk, c0, rs: blk >= cand_ref[rs, :])

        @pl.when(excess)
        def _():
            n_gt = count(lambda blk, c0, rs: blk > cand_ref[rs, :])
            need = topk - n_gt
            lane = _iota((ROW_SUB, LANES), 1)

            def count_tie_upto(cut):
                cutb = row_bcast(cut)
                return count(lambda blk, c0, rs: (blk == cand_ref[rs, :]) & ((c0 + lane) <= cutb[rs]))

            n_bits = int(seq - 1).bit_length()

            def pos_body(i, cut):
                cand = cut - jnp.left_shift(jnp.int32(1), n_bits - 1 - i)
                ok = count_tie_upto(cand) >= need
                return jnp.where(ok, cand, cut)

            cut0 = jnp.full((1, rows), (1 << n_bits) - 1, I32)
            cut = lax.fori_loop(0, n_bits, pos_body, cut0)
            cutb = row_bcast(cut)
            write_mask(lambda blk, c0, rs: (blk > cand_ref[rs, :])
                       | ((blk == cand_ref[rs, :]) & ((c0 + lane) <= cutb[rs])))

        def zero_body(j, carry):
            c0 = pl.multiple_of(j * COUNT_BLOCK, COUNT_BLOCK)
            mask_ref[:, pl.ds(c0, COUNT_BLOCK)] = jnp.zeros((rows, COUNT_BLOCK), jnp.int8)
            return carry
        lax.fori_loop(n_cnt, seq // COUNT_BLOCK, zero_body, 0)


def _select(iq, iw, ik, topk):
    s = ik.shape[0]
    qb_t, kb_t, nkb_t = _block_tables(s)
    grid_spec = pltpu.PrefetchScalarGridSpec(
        num_scalar_prefetch=3,
        grid=(len(qb_t),),
        in_specs=[pl.BlockSpec((N_HEADS, Q_BLOCK, iq.shape[2]), lambda i, qb, kb, nk: (0, qb[i], 0)),
                  pl.BlockSpec((N_HEADS, Q_BLOCK, LANES), lambda i, qb, kb, nk: (0, qb[i], 0)),
                  pl.BlockSpec((KEY_BLOCK, ik.shape[1]), lambda i, qb, kb, nk: (kb[i], 0))],
        out_specs=pl.BlockSpec((Q_BLOCK, s), lambda i, qb, kb, nk: (qb[i], 0)),
        scratch_shapes=[pltpu.VMEM((Q_BLOCK, s + LANES), F32),
                        pltpu.VMEM((1, Q_BLOCK), F32),
                        pltpu.VMEM((Q_BLOCK, LANES), F32)],
    )
    return pl.pallas_call(
        functools.partial(_select_kernel, topk=topk, seq=s),
        grid_spec=grid_spec,
        out_shape=jax.ShapeDtypeStruct((s, s), jnp.int8),
        compiler_params=pltpu.CompilerParams(dimension_semantics=("arbitrary",),
                                             vmem_limit_bytes=VMEM_LIMIT),
        name="select",
    )(jnp.asarray(qb_t), jnp.asarray(kb_t), jnp.asarray(nkb_t), iq, iw, ik)


def _attn_kernel(qb_tab, kb_tab, nkb_tab, qa_ref, qb_ref, k_ref, v_ref, mask_ref, o_ref,
                 m_ref, acc_ref, bias_ref, s_ref):
    step = pl.program_id(0)
    kb = kb_tab[step]
    nkb = nkb_tab[step]
    n_sub = KEY_BLOCK // KEY_SUB

    @pl.when(kb == 0)
    def _():
        m_ref[...] = jnp.full(m_ref.shape, NEG_BIG, F32)
        acc_ref[...] = jnp.zeros_like(acc_ref)

    for kc in range(n_sub):
        ks = slice(kc * KEY_SUB, (kc + 1) * KEY_SUB)
        bias_ref[:, ks] = (mask_ref[:, ks].astype(F32) - 1.0) * (-NEG_BIG)

    ones = jnp.ones((KEY_SUB, LANES), BF16)
    q_refs = (qa_ref, qb_ref)
    lanes_of = lambda h: slice((h // 2) * LANES, (h // 2 + 1) * LANES)
    m_old, m_new = [], []
    for h in range(N_HEADS):
        sl = lanes_of(h)
        q = q_refs[h % 2][:, sl]
        bm = None
        for kc in range(n_sub):
            ks = slice(kc * KEY_SUB, (kc + 1) * KEY_SUB)
            s = _dot_nt(q, k_ref[ks, sl]) + bias_ref[:, ks]
            s_ref[h, :, ks] = s
            mx = jnp.max(s, axis=1, keepdims=True)
            bm = mx if bm is None else jnp.maximum(bm, mx)
        m_old.append(m_ref[h])
        m_new.append(jnp.maximum(m_old[h], bm))
    for h in range(N_HEADS):
        sl = lanes_of(h)
        m2 = jnp.concatenate([m_new[h]] * (KEY_SUB // LANES), axis=1)
        pv = None
        for kc in range(n_sub):
            ks = slice(kc * KEY_SUB, (kc + 1) * KEY_SUB)
            prob = jnp.exp2(s_ref[h, :, ks] - m2).astype(BF16)
            v_aug = jnp.concatenate([v_ref[ks, sl], ones], axis=1)
            d = jnp.dot(prob, v_aug, preferred_element_type=F32)
            pv = d if pv is None else pv + d
        alpha = jnp.exp2(m_old[h] - m_new[h])
        acc_ref[h] = acc_ref[h] * jnp.concatenate([alpha, alpha], axis=1) + pv
        m_ref[h] = m_new[h]

    @pl.when(kb == nkb - 1)
    def _():
        lane_h0 = _iota((Q_BLOCK, LANES), 1) < HEAD_DIM
        for p in range(N_PAIRS):
            sl = slice(p * LANES, (p + 1) * LANES)
            a0 = acc_ref[2 * p]
            a1 = acc_ref[2 * p + 1]
            o_ref[:, sl] = jnp.where(lane_h0, a0[:, 0:LANES] / a0[:, LANES:2 * LANES],
                                     a1[:, 0:LANES] / a1[:, LANES:2 * LANES])


def _attn(qa, qb, k, v, mask):
    s = k.shape[0]
    qb_t, kb_t, nkb_t = _block_tables(s)
    qspec = pl.BlockSpec((Q_BLOCK, D_BRANCH), lambda i, qb, kb, nk: (qb[i], 0))
    kspec = pl.BlockSpec((KEY_BLOCK, D_BRANCH), lambda i, qb, kb, nk: (kb[i], 0))
    grid_spec = pltpu.PrefetchScalarGridSpec(
        num_scalar_prefetch=3,
        grid=(len(qb_t),),
        in_specs=[qspec, qspec, kspec, kspec,
                  pl.BlockSpec((Q_BLOCK, KEY_BLOCK), lambda i, qb, kb, nk: (qb[i], kb[i]))],
        out_specs=qspec,
        scratch_shapes=[pltpu.VMEM((N_HEADS, Q_BLOCK, LANES), F32),
                        pltpu.VMEM((N_HEADS, Q_BLOCK, 2 * LANES), F32),
                        pltpu.VMEM((Q_BLOCK, KEY_BLOCK), F32),
                        pltpu.VMEM((N_HEADS, Q_BLOCK, KEY_BLOCK), F32)],
    )
    return pl.pallas_call(
        _attn_kernel,
        grid_spec=grid_spec,
        out_shape=jax.ShapeDtypeStruct((s, D_BRANCH), F32),
        compiler_params=pltpu.CompilerParams(dimension_semantics=("arbitrary",),
                                             vmem_limit_bytes=VMEM_LIMIT),
        name="attn",
    )(jnp.asarray(qb_t), jnp.asarray(kb_t), jnp.asarray(nkb_t), qa, qb, k, v, mask)


def _outproj_kernel(x_ref, za_ref, yb_ref, gb_ref, pg_ref, gate_ref, wa_ref, wb_ref, wo_ref,
                    o_ref):
    d = x_ref.shape[1]
    gb = gb_ref[...]
    zb = yb_ref[...] * (gb * _sigmoid(gb))
    ya = jnp.dot(za_ref[...].astype(BF16), wa_ref[...], preferred_element_type=F32)
    yb = jnp.dot(zb.astype(BF16), wb_ref[...], preferred_element_type=F32)
    merged = _sigmoid(pg_ref[:, 0:d]) * ya + _sigmoid(pg_ref[:, d:2 * d]) * yb
    out = jnp.dot(merged.astype(BF16), wo_ref[...], preferred_element_type=F32)
    o_ref[...] = x_ref[...] + gate_ref[...] * out


def _outproj(x2, za, yb, pq, pg, gate, wa, wb, wo, tm):
    s, d = x2.shape
    row = lambda n: pl.BlockSpec((tm, n), lambda i: (i, 0))
    wsp = lambda w: pl.BlockSpec(w.shape, lambda i: (0, 0))
    gb_col = 3 * D_BRANCH // D_BRANCH
    return pl.pallas_call(
        _outproj_kernel,
        grid=(s // tm,),
        in_specs=[row(d), row(D_BRANCH), row(D_BRANCH),
                  pl.BlockSpec((tm, D_BRANCH), lambda i: (i, gb_col)),
                  row(2 * d), pl.BlockSpec((1, d), lambda i: (0, 0)),
                  wsp(wa), wsp(wb), wsp(wo)],
        out_specs=row(d),
        out_shape=jax.ShapeDtypeStruct((s, d), F32),
        compiler_params=pltpu.CompilerParams(dimension_semantics=("arbitrary",),
                                             vmem_limit_bytes=VMEM_LIMIT),
        name="outproj",
    )(x2, za, yb, pq, pg, gate, wa, wb, wo)


def _pad_cols(w, n):
    return jnp.pad(w, ((0, 0), (0, n - w.shape[1])))


def _layer(x2, c, norm_w, w_ada, b_ada, w_in, mu, w0, w_up, a0, a_up, k_k, k_a, r_k, gn_w,
           gn_b, q_gain, k_gain, w_a_out, w_b_out, w_o):
    s, d = x2.shape
    db = D_BRANCH
    assert s % KEY_BLOCK == 0 and d % LANES == 0
    topk = min(TOPK_MAX, s // 4)

    mod = _ada(jnp.broadcast_to(c, (8, d)), w_ada, b_ada[None, :])[0:1]
    shift, scale, gate = mod[:, 0:d], mod[:, d:2 * d], mod[:, 2 * d:3 * d]

    n_rwkv = 4 * db + 2 * LORA
    def rwkv_cols(w):
        return jnp.concatenate(
            [w[:, 0:3 * db], w[:, 3 * db + 2 * LORA:n_rwkv],
             _pad_cols(w[:, 3 * db:3 * db + LORA], LANES),
             _pad_cols(w[:, 3 * db + LORA:3 * db + 2 * LORA], LANES)], axis=1)
    wa = rwkv_cols(w_in[:, 0:n_rwkv]).astype(BF16)
    mu_a = rwkv_cols(mu[None, :])
    o = n_rwkv
    wq = w_in[:, o:o + 4 * db].astype(BF16)
    o += 4 * db
    w_iq = w_in[:, o:o + N_HEADS * HEAD_DIM].reshape(d, N_HEADS, 1, HEAD_DIM)
    w_iq = jnp.broadcast_to(w_iq, (d, N_HEADS, 4, HEAD_DIM)).reshape(d, N_HEADS * 4 * HEAD_DIM)
    o += N_HEADS * HEAD_DIM
    w_ik = jnp.tile(w_in[:, o:o + HEAD_DIM], (1, 4))
    o += HEAD_DIM
    w_iw = _pad_cols(w_in[:, o:o + N_HEADS], LANES)
    o += N_HEADS
    wi = jnp.concatenate([w_iq, w_ik, w_iw], axis=1).astype(BF16)
    wg = w_in[:, o:o + 2 * d].astype(BF16)

    pa, pq, pi, pg = _inproj(x2, norm_w[None, :], scale, shift, wa, wq, wi, wg, tm=256)

    pad_rows = lambda w: jnp.pad(w, ((0, LANES - w.shape[0]), (0, 0)))
    za = _rwkv(pa, mu_a, w0[None, :], a0[None, :], k_k[None, :], k_a[None, :],
               r_k.reshape(1, db), gn_w[None, :], gn_b[None, :], pad_rows(w_up), pad_rows(a_up),
               tm=512)

    qg = jnp.tile(q_gain, N_HEADS)[None, :]
    kg = jnp.tile(k_gain, N_HEADS)[None, :]
    qa, qb, kn, vb, iq, ik, iw = _dprep(pq, pi, qg, kg, tm=512)
    mask = _select(iq, iw, ik, topk)
    yb = _attn(qa, qb, kn, vb, mask)

    return _outproj(x2, za, yb, pq, pg, gate, w_a_out.astype(BF16), w_b_out.astype(BF16),
                    w_o.astype(BF16), tm=512)


def kernel(x, c, norm_w, w_ada, b_ada, w_in, mu, w0, w_up, a0, a_up, k_k, k_a, r_k, gn_w, gn_b,
           q_gain, k_gain, w_a_out, w_b_out, w_o):
    b, s, d = x.shape
    outs = []
    for bi in range(b):
        xb = x[bi]
        for l in range(norm_w.shape[0]):
            xb = _layer(xb, c[bi:bi + 1], norm_w[l], w_ada[l], b_ada[l], w_in[l], mu[l], w0[l],
                        w_up[l], a0[l], a_up[l], k_k[l], k_a[l], r_k[l], gn_w[l], gn_b[l],
                        q_gain[l], k_gain[l], w_a_out[l], w_b_out[l], w_o[l])
        outs.append(xb)
    return jnp.stack(outs, axis=0)
```

```python
import functools

import numpy as np
import jax
import jax.numpy as jnp
from jax import lax
from jax.experimental import pallas as pl
from jax.experimental.pallas import tpu as pltpu

F32 = jnp.float32
BF16 = jnp.bfloat16
I32 = jnp.int32
HIGHEST = lax.Precision.HIGHEST

CHUNK = 64
HEAD_DIM = 64
N_HEADS = 8
D_BRANCH = N_HEADS * HEAD_DIM
LORA = 64
DECAY_SCALE = 0.606531
GN_EPS = 64e-5
NORM_EPS = 1e-6
TOPK_MAX = 256
Q_BLOCK = 256

LANES = 128
N_PAIRS = D_BRANCH // LANES
VMEM_LIMIT = 56 * 1024 * 1024

KEY_BLOCK = 1024
KEY_SUB = 256
ROW_SUB = 128
COUNT_BLOCK = 1024
VALUE_ROUNDS = 20
BRACKET_SLACK = 2.0 ** -10
FLT_MAX = 3.4028234663852886e38
MIN_NORMAL_BITS = 0x00800000
MIN_NORMAL = 1.1754943508222875e-38
NEG_BIG = -1e30
LOG2E = 1.4426950408889634

NA = 4 * D_BRANCH + 2 * LANES
NI = N_HEADS * 4 * HEAD_DIM + 4 * HEAD_DIM + LANES


def _sigmoid(x):
    return 1.0 / (1.0 + jnp.exp(-x))


def _hdot(a, b):
    return jnp.dot(a, b, precision=HIGHEST, preferred_element_type=F32)


def _split(x):
    hi = x.astype(BF16)
    lo = (x - hi.astype(F32)).astype(BF16)
    return hi, lo


def _dot(a, b, dims):
    return lax.dot_general(a, b, (dims, ((), ())), preferred_element_type=F32)


_NN = ((1,), (0,))
_NT = ((1,), (1,))
_TN = ((0,), (0,))


def _mm(sa, sb, dims=_NN):
    (ah, al), (bh, bl) = sa, sb
    ca, cb = dims[0][0], dims[1][0]
    return (_dot(jnp.concatenate([ah, al], axis=ca), jnp.concatenate([bh, bh], axis=cb), dims)
            + _dot(ah, bl, dims))


def _mm_exact_rhs(sa, b, dims=_NN):
    ah, al = sa
    ca, cb = dims[0][0], dims[1][0]
    return _dot(jnp.concatenate([ah, al], axis=ca), jnp.concatenate([b, b], axis=cb), dims)


def _mm_exact_lhs(a, sb, dims=_NN):
    bh, bl = sb
    ca, cb = dims[0][0], dims[1][0]
    return _dot(jnp.concatenate([a, a], axis=ca), jnp.concatenate([bh, bl], axis=cb), dims)


def _dot_nt(a, b):
    return lax.dot_general(a, b, (((1,), (1,)), ((), ())), preferred_element_type=F32)


def _iota(shape, dim):
    return lax.broadcasted_iota(I32, shape, dim)


def _ada_kernel(c_ref, w_ref, b_ref, o_ref):
    c = c_ref[...]
    s = c * _sigmoid(c)
    o_ref[...] = _hdot(s, w_ref[...]) + b_ref[...]


def _ada(c8, w_ada, b_ada):
    d = c8.shape[1]
    n = w_ada.shape[1]
    return pl.pallas_call(
        _ada_kernel,
        grid=(n // d,),
        in_specs=[pl.BlockSpec((8, d), lambda j: (0, 0)),
                  pl.BlockSpec((d, d), lambda j: (0, j)),
                  pl.BlockSpec((1, d), lambda j: (0, j))],
        out_specs=pl.BlockSpec((8, d), lambda j: (0, j)),
        out_shape=jax.ShapeDtypeStruct((8, n), F32),
        name="ada",
    )(c8, w_ada, b_ada)


def _inproj_kernel(x_ref, nw_ref, sc_ref, sh_ref, wa_ref, wq_ref, wi_ref, wg_ref,
                   pa_ref, pq_ref, pi_ref, pg_ref):
    x = x_ref[...]
    ms = jnp.mean(x * x, axis=-1, keepdims=True)
    h = x * lax.rsqrt(ms + NORM_EPS) * nw_ref[...]
    h = h * (1.0 + sc_ref[...]) + sh_ref[...]
    hb = h.astype(BF16)
    pa_ref[...] = jnp.dot(hb, wa_ref[...], preferred_element_type=F32)
    pq_ref[...] = jnp.dot(hb, wq_ref[...], preferred_element_type=F32)
    pi_ref[...] = jnp.dot(hb, wi_ref[...], preferred_element_type=F32)
    pg_ref[...] = jnp.dot(hb, wg_ref[...], preferred_element_type=F32)


def _inproj(x2, nw, scale, shift, wa, wq, wi, wg, tm):
    s, d = x2.shape
    row = lambda n: pl.BlockSpec((tm, n), lambda i: (i, 0))
    vec = pl.BlockSpec((1, d), lambda i: (0, 0))
    wsp = lambda w: pl.BlockSpec(w.shape, lambda i: (0, 0))
    return pl.pallas_call(
        _inproj_kernel,
        grid=(s // tm,),
        in_specs=[row(d), vec, vec, vec, wsp(wa), wsp(wq), wsp(wi), wsp(wg)],
        out_specs=[row(wa.shape[1]), row(wq.shape[1]), row(wi.shape[1]), row(wg.shape[1])],
        out_shape=[jax.ShapeDtypeStruct((s, w.shape[1]), F32) for w in (wa, wq, wi, wg)],
        compiler_params=pltpu.CompilerParams(dimension_semantics=("arbitrary",),
                                             vmem_limit_bytes=VMEM_LIMIT),
        name="inproj",
    )(x2, nw, scale, shift, wa, wq, wi, wg)


def _rwkv_kernel(pa_ref, mu_ref, w0_ref, a0_ref, kk_ref, ka_ref, rk_ref, gnw_ref, gnb_ref,
                 wup_ref, aup_ref, o_ref, s_ref, carry_ref, *, n_chunks):
    @pl.when(pl.program_id(0) == 0)
    def _():
        s_ref[...] = jnp.zeros_like(s_ref)
        carry_ref[...] = jnp.zeros_like(carry_ref)

    row = _iota((LANES, LANES), 0)
    col = _iota((LANES, LANES), 1)
    strict = row > col
    incl = row >= col
    same_head = (row < HEAD_DIM) == (col < HEAD_DIM)
    bd_ones = jnp.where(same_head, 1.0, 0.0).astype(BF16)
    eye = jnp.where(row == col, 1.0, 0.0).astype(F32)
    tri = jnp.where(_iota((CHUNK, CHUNK), 0) >= _iota((CHUNK, CHUNK), 1), 1.0, 0.0).astype(BF16)
    lane_h0 = _iota((CHUNK, LANES), 1) < HEAD_DIM
    first_row = _iota((CHUNK, NA), 0) == 0
    wup = _split(wup_ref[...])
    aup = _split(aup_ref[...])

    def stack(a):
        return jnp.concatenate([jnp.where(lane_h0, a, 0.0), jnp.where(lane_h0, 0.0, a)], axis=0)

    def chunk_body(c, carry):
        r0 = pl.multiple_of(c * CHUNK, CHUNK)
        x = pa_ref[pl.ds(r0, CHUNK), :]
        prev = pltpu.roll(x, 1, 0)
        prev = jnp.where(first_row, carry_ref[7:8, :], prev)
        carry_ref[...] = x[CHUNK - 8:CHUNK, :]
        xs = x + mu_ref[...] * (prev - x)
        d = D_BRANCH
        r = xs[:, 0:d]
        k = xs[:, d:2 * d]
        v = xs[:, 2 * d:3 * d]
        g = xs[:, 3 * d:4 * d]
        wd = xs[:, 4 * d:4 * d + LANES]
        ad = xs[:, 4 * d + LANES:4 * d + 2 * LANES]
        lw = -DECAY_SCALE * _sigmoid(w0_ref[...] + _mm(_split(jnp.tanh(wd)), wup))
        a = _sigmoid(a0_ref[...] + _mm(_split(ad), aup))
        kkx = k * kk_ref[...]
        k2 = k * (1.0 + (a - 1.0) * ka_ref[...])
        bon = r * k2 * rk_ref[...]

        sls = [slice(p * LANES, (p + 1) * LANES) for p in range(N_PAIRS)]
        pre = []
        for sl in sls:
            rp, kp, lwp, kkxp = r[:, sl], k2[:, sl], lw[:, sl], kkx[:, sl]
            ss = _mm_exact_rhs(_split(kkxp * kkxp), bd_ones)
            kkp = kkxp / jnp.maximum(jnp.sqrt(ss), 1e-12)
            alpha = -kkp
            beta = kkp * a[:, sl]
            cl = _mm_exact_lhs(tri, _split(lwp))
            clp = cl - lwp
            cm = cl[CHUNK // 2 - 1:CHUNK // 2, :]
            cend = cl[CHUNK - 1:CHUNK, :]
            e_k = jnp.exp(cm - cl)
            e_h = jnp.exp(cend - cl)
            lhs = _split(jnp.concatenate([stack(alpha * jnp.exp(clp - cm)),
                                          stack(rp * jnp.exp(cl - cm))], axis=0))
            rhs = _split(jnp.concatenate([stack(beta * e_k), stack(kp * e_k)], axis=0))
            pre.append(dict(
                gram=_mm(lhs, rhs, _NT),
                a_e=_split(stack(alpha * jnp.exp(clp))),
                r_e=_split(stack(rp * jnp.exp(cl))),
                bk=_split(jnp.concatenate([stack(beta * e_h), stack(kp * e_h)], axis=0)),
                v_st=_split(stack(v[:, sl])),
                g_end=jnp.exp(cend)))
        n_ab = [jnp.where(strict, q["gram"][0:LANES, 0:LANES], 0.0) for q in pre]
        tinv = [eye + n for n in n_ab]
        npow = n_ab
        for _ in range(5):
            sps = [_split(n) for n in npow]
            npow = [_mm(sp, sp) for sp in sps]
            tinv = [t + _mm(_split(t), _split(n)) for t, n in zip(tinv, npow)]
        s0 = [s_ref[p] for p in range(N_PAIRS)]
        s0s = [_split(x0) for x0 in s0]
        u_rhs = [_mm(q["a_e"], ss0, _NT)
                 + _mm(_split(jnp.where(strict, q["gram"][0:LANES, LANES:2 * LANES], 0.0)), q["v_st"])
                 for q, ss0 in zip(pre, s0s)]
        u_sp = [_split(_mm(_split(t), _split(u))) for t, u in zip(tinv, u_rhs)]
        for p, (q, us) in enumerate(zip(pre, u_sp)):
            uv = (jnp.concatenate([us[0], q["v_st"][0]], axis=0),
                  jnp.concatenate([us[1], q["v_st"][1]], axis=0))
            s_ref[p] = s0[p] * q["g_end"] + _mm(uv, q["bk"], _TN)
        ys = []
        for q, ss0, us in zip(pre, s0s, u_sp):
            m_rb = jnp.where(incl, q["gram"][LANES:2 * LANES, 0:LANES], 0.0)
            m_rk = jnp.where(incl, q["gram"][LANES:2 * LANES, LANES:2 * LANES], 0.0)
            y_st = _mm(q["r_e"], ss0, _NT) + _mm(_split(m_rb), us) + _mm(_split(m_rk), q["v_st"])
            ys.append(y_st[0:CHUNK] + y_st[CHUNK:2 * CHUNK])
        bonus = [_mm_exact_rhs(_split(bon[:, sl]), bd_ones) * v[:, sl] for sl in sls]
        dlt = [y - _mm_exact_rhs(_split(y), bd_ones) * (1.0 / HEAD_DIM) for y in ys]
        var = [_mm_exact_rhs(_split(dl * dl), bd_ones) * (1.0 / HEAD_DIM) for dl in dlt]
        for sl, dl, vr, bo in zip(sls, dlt, var, bonus):
            yn = dl * lax.rsqrt(vr + GN_EPS) * gnw_ref[:, sl] + gnb_ref[:, sl]
            gp = g[:, sl]
            o_ref[pl.ds(r0, CHUNK), sl] = (yn + bo) * (gp * _sigmoid(gp))
        return carry

    lax.fori_loop(0, n_chunks, chunk_body, 0)


def _rwkv(pa, mu, w0, a0, k_k, k_a, r_k, gn_w, gn_b, w_up, a_up, tm):
    s = pa.shape[0]
    vec = lambda n: pl.BlockSpec((1, n), lambda i: (0, 0))
    lora = pl.BlockSpec((LANES, D_BRANCH), lambda i: (0, 0))
    return pl.pallas_call(
        functools.partial(_rwkv_kernel, n_chunks=tm // CHUNK),
        grid=(s // tm,),
        in_specs=[pl.BlockSpec((tm, NA), lambda i: (i, 0)), vec(NA)] + [vec(D_BRANCH)] * 7
                 + [lora, lora],
        out_specs=pl.BlockSpec((tm, D_BRANCH), lambda i: (i, 0)),
        out_shape=jax.ShapeDtypeStruct((s, D_BRANCH), F32),
        scratch_shapes=[pltpu.VMEM((N_PAIRS, LANES, LANES), F32),
                        pltpu.VMEM((8, NA), F32)],
        compiler_params=pltpu.CompilerParams(dimension_semantics=("arbitrary",),
                                             vmem_limit_bytes=VMEM_LIMIT),
        name="rwkv",
    )(pa, mu, w0, a0, k_k, k_a, r_k, gn_w, gn_b, w_up, a_up)


def _split_hi_lo(x, take_hi):
    hi = x.astype(BF16)
    lo = (x - hi.astype(F32)).astype(BF16)
    return jnp.where(take_hi, hi, lo)


def _dprep_kernel(pq_ref, pi_ref, qg_ref, kg_ref, qa_ref, qb_ref, k_ref, v_ref,
                  iq_ref, ik_ref, iw_ref):
    tm = pq_ref.shape[0]
    d = D_BRANCH
    row = _iota((LANES, LANES), 0)
    col = _iota((LANES, LANES), 1)
    bd_avg = jnp.where((row < HEAD_DIM) == (col < HEAD_DIM), 1.0 / HEAD_DIM, 0.0).astype(F32)
    lane_h0 = _iota((tm, LANES), 1) < HEAD_DIM
    q_scale = (HEAD_DIM ** -0.5) * LOG2E
    for p in range(N_PAIRS):
        sl = slice(p * LANES, (p + 1) * LANES)
        q = pq_ref[:, p * LANES:(p + 1) * LANES]
        k = pq_ref[:, d + p * LANES:d + (p + 1) * LANES]
        qn = q * lax.rsqrt(_hdot(q * q, bd_avg) + NORM_EPS) * qg_ref[:, sl]
        kn = k * lax.rsqrt(_hdot(k * k, bd_avg) + NORM_EPS) * kg_ref[:, sl]
        qs = (qn * q_scale).astype(BF16)
        qa_ref[:, sl] = jnp.where(lane_h0, qs, jnp.zeros_like(qs))
        qb_ref[:, sl] = jnp.where(lane_h0, jnp.zeros_like(qs), qs)
        k_ref[:, sl] = kn.astype(BF16)
    v_ref[...] = pq_ref[:, 2 * d:3 * d].astype(BF16)
    lane4 = _iota((tm, 4 * HEAD_DIM), 1) // HEAD_DIM
    q_hi = (lane4 % 2) == 0
    k_hi = lane4 < 2
    off = N_HEADS * 4 * HEAD_DIM
    iw = pi_ref[:, off + 4 * HEAD_DIM:off + 4 * HEAD_DIM + LANES] * (
        (N_HEADS ** -0.5) * (HEAD_DIM ** -0.5))
    for h in range(N_HEADS):
        sl = slice(h * 4 * HEAD_DIM, (h + 1) * 4 * HEAD_DIM)
        iq_ref[h] = _split_hi_lo(pi_ref[:, sl], q_hi)
        iw_ref[h] = jnp.broadcast_to(iw[:, h:h + 1], (tm, LANES))
    ik_ref[...] = _split_hi_lo(pi_ref[:, off:off + 4 * HEAD_DIM], k_hi)


def _dprep(pq, pi, qg, kg, tm):
    s = pq.shape[0]
    row = lambda n: pl.BlockSpec((tm, n), lambda i: (i, 0))
    hrow = lambda n: pl.BlockSpec((N_HEADS, tm, n), lambda i: (0, i, 0))
    vec = pl.BlockSpec((1, D_BRANCH), lambda i: (0, 0))
    kq = 4 * HEAD_DIM
    return pl.pallas_call(
        _dprep_kernel,
        grid=(s // tm,),
        in_specs=[row(pq.shape[1]), row(pi.shape[1]), vec, vec],
        out_specs=[row(D_BRANCH)] * 4 + [hrow(kq), row(kq), hrow(LANES)],
        out_shape=[jax.ShapeDtypeStruct((s, D_BRANCH), BF16)] * 4
                  + [jax.ShapeDtypeStruct((N_HEADS, s, kq), BF16),
                     jax.ShapeDtypeStruct((s, kq), BF16),
                     jax.ShapeDtypeStruct((N_HEADS, s, LANES), F32)],
        compiler_params=pltpu.CompilerParams(dimension_semantics=("arbitrary",),
                                             vmem_limit_bytes=VMEM_LIMIT),
        name="dprep",
    )(pq, pi, qg, kg)


def _block_tables(s):
    qb_l, kb_l, nkb_l = [], [], []
    for qb in range(s // Q_BLOCK):
        nkb = -(-(Q_BLOCK * (qb + 1)) // KEY_BLOCK)
        for kb in range(nkb):
            qb_l.append(qb)
            kb_l.append(kb)
            nkb_l.append(nkb)
    return (np.asarray(qb_l, np.int32), np.asarray(kb_l, np.int32), np.asarray(nkb_l, np.int32))


def _float_to_ckey(x):
    b = pltpu.bitcast(x, I32)
    k = b ^ ((b >> 31) & 0x7FFFFFFF)
    return jnp.where(k >= MIN_NORMAL_BITS, k - (MIN_NORMAL_BITS - 1),
                     jnp.where(k < -MIN_NORMAL_BITS, k + MIN_NORMAL_BITS, 0))


def _ckey_to_float(c):
    k = jnp.where(c > 0, c + (MIN_NORMAL_BITS - 1), jnp.where(c < 0, c - MIN_NORMAL_BITS, 0))
    return pltpu.bitcast(k ^ ((k >> 31) & 0x7FFFFFFF), F32)


def _select_kernel(qb_tab, kb_tab, nkb_tab, iq_ref, iw_ref, ik_ref, mask_ref, sc_ref, cnt_ref,
                   cand_ref, bst_ref, lok_ref, *, topk, seq):
    step = pl.program_id(0)
    qb = qb_tab[step]
    kb = kb_tab[step]
    nkb = nkb_tab[step]
    rows = Q_BLOCK

    lane_pos = _iota((ROW_SUB, KEY_SUB), 1)
    for rh in range(rows // ROW_SUB):
        rs = slice(rh * ROW_SUB, (rh + 1) * ROW_SUB)
        q_chunk = (qb * Q_BLOCK + rh * ROW_SUB + _iota((ROW_SUB, KEY_SUB), 0)) // CHUNK
        iq_all = iq_ref[:, rs, :].reshape(N_HEADS * ROW_SUB, 4 * HEAD_DIM)
        for kc in range(KEY_BLOCK // KEY_SUB):
            sc = _dot_nt(iq_all, ik_ref[kc * KEY_SUB:(kc + 1) * KEY_SUB, :])
            acc = jnp.zeros((ROW_SUB, KEY_SUB), F32)
            for h in range(N_HEADS):
                w = iw_ref[h, rs, :]
                acc = acc + jnp.maximum(sc[h * ROW_SUB:(h + 1) * ROW_SUB], 0.0) * jnp.concatenate(
                    [w] * (KEY_SUB // LANES), axis=1)
            base = kb * KEY_BLOCK + kc * KEY_SUB
            adm = ((base + lane_pos) // CHUNK) <= q_chunk
            sc_ref[rs, pl.ds(pl.multiple_of(base, KEY_SUB), KEY_SUB)] = jnp.where(adm, acc, -jnp.inf)

    @pl.when(kb == nkb - 1)
    def _():
        n_cnt = nkb * (KEY_BLOCK // COUNT_BLOCK)
        n_lb = COUNT_BLOCK // LANES

        def sweep(fn, init):
            outs = []
            for rh in range(rows // ROW_SUB):
                rs = slice(rh * ROW_SUB, (rh + 1) * ROW_SUB)

                def body(j, carry, rs=rs):
                    c0 = pl.multiple_of(j * COUNT_BLOCK, COUNT_BLOCK)
                    blk = sc_ref[rs, pl.ds(c0, COUNT_BLOCK)]
                    for i in range(n_lb):
                        carry = fn(carry, blk[:, i * LANES:(i + 1) * LANES], c0 + i * LANES, i % 2, rs)
                    return carry
                outs.append(lax.fori_loop(0, n_cnt, body, init))
            return jax.tree.map(lambda *xs: jnp.concatenate(xs, axis=0), *outs)

        def lane_reduce(x, red):
            return red(jnp.transpose(x), axis=0, keepdims=True)

        def row_bcast(x):
            return jnp.transpose(jnp.broadcast_to(x, (LANES, rows)))

        def count(pred):
            part = sweep(lambda part, blk, c0, par, rs: jnp.where(pred(blk, c0, rs), part + 1.0, part),
                         jnp.zeros((ROW_SUB, LANES), F32))
            return lane_reduce(part, jnp.sum)

        def count_ge(cand):
            cand_ref[...] = row_bcast(cand)
            return count(lambda blk, c0, rs: blk >= cand_ref[rs, :])

        neg = jnp.full((ROW_SUB, LANES), -jnp.inf, F32)
        g0, g1 = sweep(lambda g, blk, c0, par, rs: ((jnp.maximum(g[0], blk), g[1]) if par == 0
                                                     else (g[0], jnp.maximum(g[1], blk))), (neg, neg))
        row_max = lane_reduce(jnp.maximum(g0, g1), jnp.max)
        lo = lane_reduce(jnp.minimum(g0, g1), jnp.min)
        lo = _ckey_to_float(_float_to_ckey(jnp.maximum(lo - jnp.abs(lo) * BRACKET_SLACK, -FLT_MAX)))
        hi = _ckey_to_float(_float_to_ckey(row_max) + 1)
        cnt = count_ge(lo)
        short = jnp.where(cnt < topk, 1, 0)

        @pl.when(jnp.max(short) > 0)
        def _():
            cnt_ref[...] = count_ge(jnp.full((1, rows), -FLT_MAX, F32))

        @pl.when(jnp.max(short) == 0)
        def _():
            cnt_ref[...] = cnt

        lo = jnp.where(short > 0, -FLT_MAX, lo)
        cnt = jnp.where(short > 0, cnt_ref[...], cnt)
        act = jnp.where(cnt > topk, 1, 0)

        def cond(st):
            return st[5] > 0

        def body(st):
            lo, hi, cnt, act, rnd, _ = st
            lk, hk = _float_to_ckey(lo), _float_to_ckey(hi)
            mid_k = _ckey_to_float((lk >> 1) + (hk >> 1) + (lk & hk & 1))
            mid_v = _ckey_to_float(_float_to_ckey(0.5 * lo + 0.5 * hi))
            use_v = jnp.where(rnd < VALUE_ROUNDS, 1, 0) * jnp.where(mid_v > lo, 1, 0) * jnp.where(
                mid_v < hi, 1, 0)
            mid = jnp.where(use_v > 0, mid_v, mid_k)
            straddle = jnp.where(lo < 0.0, 1, 0) * jnp.where(hi > 0.0, 1, 0)
            at_zero = jnp.where(lo == 0.0, 1, 0) * jnp.where(hi > MIN_NORMAL, 1, 0)
            mid = jnp.where(straddle > 0, 0.0, jnp.where(at_zero > 0, MIN_NORMAL, mid))
            inside = act * jnp.where(mid > lo, 1, 0) * jnp.where(mid < hi, 1, 0)
            c = count_ge(mid)
            up = inside * jnp.where(c >= topk, 1, 0)
            dn = inside - up
            lo = jnp.where(up > 0, mid, lo)
            cnt = jnp.where(up > 0, c, cnt)
            hi = jnp.where(dn > 0, mid, hi)
            act = inside * jnp.where(cnt > topk, 1, 0)
            return lo, hi, cnt, act, rnd + 1, jnp.max(act)

        lo, hi, cnt, act, _, _ = lax.while_loop(
            cond, body, (lo, hi, cnt, act, jnp.int32(0), jnp.max(act)))
        cand_ref[...] = row_bcast(lo)
        excess = jnp.max(cnt) > topk

        def write_mask(sel):
            for rh in range(rows // ROW_SUB):
                rs = slice(rh * ROW_SUB, (rh + 1) * ROW_SUB)

                def body(j, carry, rs=rs):
                    c0 = pl.multiple_of(j * COUNT_BLOCK, COUNT_BLOCK)
                    blk = sc_ref[rs, pl.ds(c0, COUNT_BLOCK)]
                    out = jnp.concatenate(
                        [jnp.where(sel(blk[:, i * LANES:(i + 1) * LANES], c0 + i * LANES, rs), 1.0, 0.0)
                         for i in range(n_lb)], axis=1)
                    mask_ref[rs, pl.ds(c0, COUNT_BLOCK)] = out.astype(jnp.int8)
                    return carry
                lax.fori_loop(0, n_cnt, body, 0)

        @pl.when(jnp.logical_not(excess))
        def _():
            write_mask(lambda blk, c0, rs: blk >= cand_ref[rs, :])

        @pl.when(excess)
        def _():
            n_gt = count(lambda blk, c0, rs: blk > cand_ref[rs, :])
            need = row_bcast(topk - n_gt)
            lane = _iota((ROW_SUB, LANES), 1)
            lane_all = _iota((rows, LANES), 1)

            def tie(blk, rs):
                return jnp.where(blk == cand_ref[rs, :], 1.0, 0.0)

            def block_of(c0):
                return lax.shift_right_logical(c0, LANES.bit_length() - 1)

            def running_sum(x):
                sh = 1
                while sh < LANES:
                    x = x + jnp.where(lane_all >= sh, pltpu.roll(x, sh, 1), 0.0)
                    sh *= 2
                return x

            ties = sweep(lambda t, blk, c0, par, rs: jnp.where(
                lane == block_of(c0), jnp.sum(tie(blk, rs), axis=1, keepdims=True), t),
                jnp.zeros((ROW_SUB, LANES), F32))
            whole = jnp.where(running_sum(ties) < need, 1.0, 0.0)
            bst_ref[...] = jnp.broadcast_to(jnp.sum(whole, axis=1, keepdims=True), (rows, LANES))
            left = need - jnp.sum(whole * ties, axis=1, keepdims=True)
            pat = sweep(lambda g, blk, c0, par, rs: jnp.where(
                bst_ref[rs, :] == block_of(c0).astype(F32), tie(blk, rs), g),
                jnp.zeros((ROW_SUB, LANES), F32))
            lok_ref[...] = jnp.where(running_sum(pat) <= left, 1.0, 0.0)

            def sel(blk, c0, rs):
                b = block_of(c0).astype(F32)
                part = (bst_ref[rs, :] == b) & (lok_ref[rs, :] > 0.0)
                return (blk > cand_ref[rs, :]) | ((blk == cand_ref[rs, :]) & ((bst_ref[rs, :] > b) | part))
            write_mask(sel)

        def zero_body(j, carry):
            c0 = pl.multiple_of(j * COUNT_BLOCK, COUNT_BLOCK)
            mask_ref[:, pl.ds(c0, COUNT_BLOCK)] = jnp.zeros((rows, COUNT_BLOCK), jnp.int8)
            return carry
        lax.fori_loop(n_cnt, seq // COUNT_BLOCK, zero_body, 0)


def _select(iq, iw, ik, topk):
    s = ik.shape[0]
    assert s // LANES <= LANES
    qb_t, kb_t, nkb_t = _block_tables(s)
    grid_spec = pltpu.PrefetchScalarGridSpec(
        num_scalar_prefetch=3,
        grid=(len(qb_t),),
        in_specs=[pl.BlockSpec((N_HEADS, Q_BLOCK, iq.shape[2]), lambda i, qb, kb, nk: (0, qb[i], 0)),
                  pl.BlockSpec((N_HEADS, Q_BLOCK, LANES), lambda i, qb, kb, nk: (0, qb[i], 0)),
                  pl.BlockSpec((KEY_BLOCK, ik.shape[1]), lambda i, qb, kb, nk: (kb[i], 0))],
        out_specs=pl.BlockSpec((Q_BLOCK, s), lambda i, qb, kb, nk: (qb[i], 0)),
        scratch_shapes=[pltpu.VMEM((Q_BLOCK, s + LANES), F32),
                        pltpu.VMEM((1, Q_BLOCK), F32),
                        pltpu.VMEM((Q_BLOCK, LANES), F32),
                        pltpu.VMEM((Q_BLOCK, LANES), F32),
                        pltpu.VMEM((Q_BLOCK, LANES), F32)],
    )
    return pl.pallas_call(
        functools.partial(_select_kernel, topk=topk, seq=s),
        grid_spec=grid_spec,
        out_shape=jax.ShapeDtypeStruct((s, s), jnp.int8),
        compiler_params=pltpu.CompilerParams(dimension_semantics=("arbitrary",),
                                             vmem_limit_bytes=VMEM_LIMIT),
        name="select",
    )(jnp.asarray(qb_t), jnp.asarray(kb_t), jnp.asarray(nkb_t), iq, iw, ik)


def _attn_kernel(qb_tab, kb_tab, nkb_tab, qa_ref, qb_ref, k_ref, v_ref, mask_ref, o_ref,
                 m_ref, acc_ref, bias_ref, s_ref):
    step = pl.program_id(0)
    kb = kb_tab[step]
    nkb = nkb_tab[step]
    n_sub = KEY_BLOCK // KEY_SUB

    @pl.when(kb == 0)
    def _():
        m_ref[...] = jnp.full(m_ref.shape, NEG_BIG, F32)
        acc_ref[...] = jnp.zeros_like(acc_ref)

    for kc in range(n_sub):
        ks = slice(kc * KEY_SUB, (kc + 1) * KEY_SUB)
        bias_ref[:, ks] = (mask_ref[:, ks].astype(F32) - 1.0) * (-NEG_BIG)

    ones = jnp.ones((KEY_SUB, LANES), BF16)
    q_refs = (qa_ref, qb_ref)
    lanes_of = lambda h: slice((h // 2) * LANES, (h // 2 + 1) * LANES)
    m_old, m_new = [], []
    for h in range(N_HEADS):
        sl = lanes_of(h)
        q = q_refs[h % 2][:, sl]
        bm = None
        for kc in range(n_sub):
            ks = slice(kc * KEY_SUB, (kc + 1) * KEY_SUB)
            s = _dot_nt(q, k_ref[ks, sl]) + bias_ref[:, ks]
            s_ref[h, :, ks] = s
            mx = jnp.max(s, axis=1, keepdims=True)
            bm = mx if bm is None else jnp.maximum(bm, mx)
        m_old.append(m_ref[h])
        m_new.append(jnp.maximum(m_old[h], bm))
    for h in range(N_HEADS):
        sl = lanes_of(h)
        m2 = jnp.concatenate([m_new[h]] * (KEY_SUB // LANES), axis=1)
        pv = None
        for kc in range(n_sub):
            ks = slice(kc * KEY_SUB, (kc + 1) * KEY_SUB)
            prob = jnp.exp2(s_ref[h, :, ks] - m2).astype(BF16)
            v_aug = jnp.concatenate([v_ref[ks, sl], ones], axis=1)
            d = jnp.dot(prob, v_aug, preferred_element_type=F32)
            pv = d if pv is None else pv + d
        alpha = jnp.exp2(m_old[h] - m_new[h])
        acc_ref[h] = acc_ref[h] * jnp.concatenate([alpha, alpha], axis=1) + pv
        m_ref[h] = m_new[h]

    @pl.when(kb == nkb - 1)
    def _():
        lane_h0 = _iota((Q_BLOCK, LANES), 1) < HEAD_DIM
        for p in range(N_PAIRS):
            sl = slice(p * LANES, (p + 1) * LANES)
            a0 = acc_ref[2 * p]
            a1 = acc_ref[2 * p + 1]
            o_ref[:, sl] = jnp.where(lane_h0, a0[:, 0:LANES] / a0[:, LANES:2 * LANES],
                                     a1[:, 0:LANES] / a1[:, LANES:2 * LANES])


def _attn(qa, qb, k, v, mask):
    s = k.shape[0]
    qb_t, kb_t, nkb_t = _block_tables(s)
    qspec = pl.BlockSpec((Q_BLOCK, D_BRANCH), lambda i, qb, kb, nk: (qb[i], 0))
    kspec = pl.BlockSpec((KEY_BLOCK, D_BRANCH), lambda i, qb, kb, nk: (kb[i], 0))
    grid_spec = pltpu.PrefetchScalarGridSpec(
        num_scalar_prefetch=3,
        grid=(len(qb_t),),
        in_specs=[qspec, qspec, kspec, kspec,
                  pl.BlockSpec((Q_BLOCK, KEY_BLOCK), lambda i, qb, kb, nk: (qb[i], kb[i]))],
        out_specs=qspec,
        scratch_shapes=[pltpu.VMEM((N_HEADS, Q_BLOCK, LANES), F32),
                        pltpu.VMEM((N_HEADS, Q_BLOCK, 2 * LANES), F32),
                        pltpu.VMEM((Q_BLOCK, KEY_BLOCK), F32),
                        pltpu.VMEM((N_HEADS, Q_BLOCK, KEY_BLOCK), F32)],
    )
    return pl.pallas_call(
        _attn_kernel,
        grid_spec=grid_spec,
        out_shape=jax.ShapeDtypeStruct((s, D_BRANCH), F32),
        compiler_params=pltpu.CompilerParams(dimension_semantics=("arbitrary",),
                                             vmem_limit_bytes=VMEM_LIMIT),
        name="attn",
    )(jnp.asarray(qb_t), jnp.asarray(kb_t), jnp.asarray(nkb_t), qa, qb, k, v, mask)


def _outproj_kernel(x_ref, za_ref, yb_ref, gb_ref, pg_ref, gate_ref, wa_ref, wb_ref, wo_ref,
                    o_ref):
    d = x_ref.shape[1]
    gb = gb_ref[...]
    zb = yb_ref[...] * (gb * _sigmoid(gb))
    ya = jnp.dot(za_ref[...].astype(BF16), wa_ref[...], preferred_element_type=F32)
    yb = jnp.dot(zb.astype(BF16), wb_ref[...], preferred_element_type=F32)
    merged = _sigmoid(pg_ref[:, 0:d]) * ya + _sigmoid(pg_ref[:, d:2 * d]) * yb
    out = jnp.dot(merged.astype(BF16), wo_ref[...], preferred_element_type=F32)
    o_ref[...] = x_ref[...] + gate_ref[...] * out


def _outproj(x2, za, yb, pq, pg, gate, wa, wb, wo, tm):
    s, d = x2.shape
    row = lambda n: pl.BlockSpec((tm, n), lambda i: (i, 0))
    wsp = lambda w: pl.BlockSpec(w.shape, lambda i: (0, 0))
    gb_col = 3 * D_BRANCH // D_BRANCH
    return pl.pallas_call(
        _outproj_kernel,
        grid=(s // tm,),
        in_specs=[row(d), row(D_BRANCH), row(D_BRANCH),
                  pl.BlockSpec((tm, D_BRANCH), lambda i: (i, gb_col)),
                  row(2 * d), pl.BlockSpec((1, d), lambda i: (0, 0)),
                  wsp(wa), wsp(wb), wsp(wo)],
        out_specs=row(d),
        out_shape=jax.ShapeDtypeStruct((s, d), F32),
        compiler_params=pltpu.CompilerParams(dimension_semantics=("arbitrary",),
                                             vmem_limit_bytes=VMEM_LIMIT),
        name="outproj",
    )(x2, za, yb, pq, pg, gate, wa, wb, wo)


def _pad_cols(w, n):
    return jnp.pad(w, ((0, 0), (0, n - w.shape[1])))


def _layer(x2, c, norm_w, w_ada, b_ada, w_in, mu, w0, w_up, a0, a_up, k_k, k_a, r_k, gn_w,
           gn_b, q_gain, k_gain, w_a_out, w_b_out, w_o):
    s, d = x2.shape
    db = D_BRANCH
    assert s % KEY_BLOCK == 0 and d % LANES == 0
    topk = min(TOPK_MAX, s // 4)

    mod = _ada(jnp.broadcast_to(c, (8, d)), w_ada, b_ada[None, :])[0:1]
    shift, scale, gate = mod[:, 0:d], mod[:, d:2 * d], mod[:, 2 * d:3 * d]

    n_rwkv = 4 * db + 2 * LORA
    def rwkv_cols(w):
        return jnp.concatenate(
            [w[:, 0:3 * db], w[:, 3 * db + 2 * LORA:n_rwkv],
             _pad_cols(w[:, 3 * db:3 * db + LORA], LANES),
             _pad_cols(w[:, 3 * db + LORA:3 * db + 2 * LORA], LANES)], axis=1)
    wa = rwkv_cols(w_in[:, 0:n_rwkv]).astype(BF16)
    mu_a = rwkv_cols(mu[None, :])
    o = n_rwkv
    wq = w_in[:, o:o + 4 * db].astype(BF16)
    o += 4 * db
    w_iq = w_in[:, o:o + N_HEADS * HEAD_DIM].reshape(d, N_HEADS, 1, HEAD_DIM)
    w_iq = jnp.broadcast_to(w_iq, (d, N_HEADS, 4, HEAD_DIM)).reshape(d, N_HEADS * 4 * HEAD_DIM)
    o += N_HEADS * HEAD_DIM
    w_ik = jnp.tile(w_in[:, o:o + HEAD_DIM], (1, 4))
    o += HEAD_DIM
    w_iw = _pad_cols(w_in[:, o:o + N_HEADS], LANES)
    o += N_HEADS
    wi = jnp.concatenate([w_iq, w_ik, w_iw], axis=1).astype(BF16)
    wg = w_in[:, o:o + 2 * d].astype(BF16)

    pa, pq, pi, pg = _inproj(x2, norm_w[None, :], scale, shift, wa, wq, wi, wg, tm=256)

    pad_rows = lambda w: jnp.pad(w, ((0, LANES - w.shape[0]), (0, 0)))
    za = _rwkv(pa, mu_a, w0[None, :], a0[None, :], k_k[None, :], k_a[None, :],
               r_k.reshape(1, db), gn_w[None, :], gn_b[None, :], pad_rows(w_up), pad_rows(a_up),
               tm=512)

    qg = jnp.tile(q_gain, N_HEADS)[None, :]
    kg = jnp.tile(k_gain, N_HEADS)[None, :]
    qa, qb, kn, vb, iq, ik, iw = _dprep(pq, pi, qg, kg, tm=512)
    mask = _select(iq, iw, ik, topk)
    yb = _attn(qa, qb, kn, vb, mask)

    return _outproj(x2, za, yb, pq, pg, gate, w_a_out.astype(BF16), w_b_out.astype(BF16),
                    w_o.astype(BF16), tm=512)


def kernel(x, c, norm_w, w_ada, b_ada, w_in, mu, w0, w_up, a0, a_up, k_k, k_a, r_k, gn_w, gn_b,
           q_gain, k_gain, w_a_out, w_b_out, w_o):
    b, s, d = x.shape
    outs = []
    for bi in range(b):
        xb = x[bi]
        for l in range(norm_w.shape[0]):
            xb = _layer(xb, c[bi:bi + 1], norm_w[l], w_ada[l], b_ada[l], w_in[l], mu[l], w0[l],
                        w_up[l], a0[l], a_up[l], k_k[l], k_a[l], r_k[l], gn_w[l], gn_b[l],
                        q_gain[l], k_gain[l], w_a_out[l], w_b_out[l], w_o[l])
        outs.append(xb)
    return jnp.stack(outs, axis=0)
```

```python
import functools

import numpy as np
import jax
import jax.numpy as jnp
from jax import lax
from jax.experimental import pallas as pl
from jax.experimental.pallas import tpu as pltpu

F32 = jnp.float32
BF16 = jnp.bfloat16
I32 = jnp.int32
HIGHEST = lax.Precision.HIGHEST

CHUNK = 64
HEAD_DIM = 64
N_HEADS = 8
D_BRANCH = N_HEADS * HEAD_DIM
LORA = 64
DECAY_SCALE = 0.606531
GN_EPS = 64e-5
NORM_EPS = 1e-6
TOPK_MAX = 256
Q_BLOCK = 256

LANES = 128
N_PAIRS = D_BRANCH // LANES
VMEM_LIMIT = 56 * 1024 * 1024

KEY_BLOCK = 1024
KEY_SUB = 256
ROW_SUB = 128
COUNT_BLOCK = 1024
VALUE_ROUNDS = 20
BRACKET_SLACK = 2.0 ** -10
PIVOT_CLIP = 0.1
PIVOT_MIN_SPAN = 8.0
FLT_MAX = 3.4028234663852886e38
MIN_NORMAL_BITS = 0x00800000
MIN_NORMAL = 1.1754943508222875e-38
NEG_BIG = -1e30
LOG2E = 1.4426950408889634

NA = 4 * D_BRANCH + 2 * LANES
NI = N_HEADS * 4 * HEAD_DIM + 4 * HEAD_DIM + LANES


def _sigmoid(x):
    return 1.0 / (1.0 + jnp.exp(-x))


def _hdot(a, b):
    return jnp.dot(a, b, precision=HIGHEST, preferred_element_type=F32)


def _split(x):
    hi = x.astype(BF16)
    lo = (x - hi.astype(F32)).astype(BF16)
    return hi, lo


def _dot(a, b, dims):
    return lax.dot_general(a, b, (dims, ((), ())), preferred_element_type=F32)


_NN = ((1,), (0,))
_NT = ((1,), (1,))
_TN = ((0,), (0,))


def _mm(sa, sb, dims=_NN):
    (ah, al), (bh, bl) = sa, sb
    ca, cb = dims[0][0], dims[1][0]
    return (_dot(jnp.concatenate([ah, al], axis=ca), jnp.concatenate([bh, bh], axis=cb), dims)
            + _dot(ah, bl, dims))


def _mm_exact_rhs(sa, b, dims=_NN):
    ah, al = sa
    ca, cb = dims[0][0], dims[1][0]
    return _dot(jnp.concatenate([ah, al], axis=ca), jnp.concatenate([b, b], axis=cb), dims)


def _mm_exact_lhs(a, sb, dims=_NN):
    bh, bl = sb
    ca, cb = dims[0][0], dims[1][0]
    return _dot(jnp.concatenate([a, a], axis=ca), jnp.concatenate([bh, bl], axis=cb), dims)


def _dot_nt(a, b):
    return lax.dot_general(a, b, (((1,), (1,)), ((), ())), preferred_element_type=F32)


def _iota(shape, dim):
    return lax.broadcasted_iota(I32, shape, dim)


def _ada_kernel(c_ref, w_ref, b_ref, o_ref):
    c = c_ref[...]
    s = c * _sigmoid(c)
    o_ref[...] = _hdot(s, w_ref[...]) + b_ref[...]


def _ada(c8, w_ada, b_ada):
    d = c8.shape[1]
    n = w_ada.shape[1]
    return pl.pallas_call(
        _ada_kernel,
        grid=(n // d,),
        in_specs=[pl.BlockSpec((8, d), lambda j: (0, 0)),
                  pl.BlockSpec((d, d), lambda j: (0, j)),
                  pl.BlockSpec((1, d), lambda j: (0, j))],
        out_specs=pl.BlockSpec((8, d), lambda j: (0, j)),
        out_shape=jax.ShapeDtypeStruct((8, n), F32),
        name="ada",
    )(c8, w_ada, b_ada)


def _inproj_kernel(x_ref, nw_ref, sc_ref, sh_ref, wa_ref, wq_ref, wi_ref, wg_ref,
                   pa_ref, pq_ref, pi_ref, pg_ref):
    x = x_ref[...]
    ms = jnp.mean(x * x, axis=-1, keepdims=True)
    h = x * lax.rsqrt(ms + NORM_EPS) * nw_ref[...]
    h = h * (1.0 + sc_ref[...]) + sh_ref[...]
    hb = h.astype(BF16)
    pa_ref[...] = jnp.dot(hb, wa_ref[...], preferred_element_type=F32)
    pq_ref[...] = jnp.dot(hb, wq_ref[...], preferred_element_type=F32)
    pi_ref[...] = jnp.dot(hb, wi_ref[...], preferred_element_type=F32)
    pg_ref[...] = jnp.dot(hb, wg_ref[...], preferred_element_type=F32)


def _inproj(x2, nw, scale, shift, wa, wq, wi, wg, tm):
    s, d = x2.shape
    row = lambda n: pl.BlockSpec((tm, n), lambda i: (i, 0))
    vec = pl.BlockSpec((1, d), lambda i: (0, 0))
    wsp = lambda w: pl.BlockSpec(w.shape, lambda i: (0, 0))
    return pl.pallas_call(
        _inproj_kernel,
        grid=(s // tm,),
        in_specs=[row(d), vec, vec, vec, wsp(wa), wsp(wq), wsp(wi), wsp(wg)],
        out_specs=[row(wa.shape[1]), row(wq.shape[1]), row(wi.shape[1]), row(wg.shape[1])],
        out_shape=[jax.ShapeDtypeStruct((s, w.shape[1]), F32) for w in (wa, wq, wi, wg)],
        compiler_params=pltpu.CompilerParams(dimension_semantics=("arbitrary",),
                                             vmem_limit_bytes=VMEM_LIMIT),
        name="inproj",
    )(x2, nw, scale, shift, wa, wq, wi, wg)


def _rwkv_kernel(pa_ref, mu_ref, w0_ref, a0_ref, kk_ref, ka_ref, rk_ref, gnw_ref, gnb_ref,
                 wup_ref, aup_ref, o_ref, s_ref, carry_ref, *, n_chunks):
    @pl.when(pl.program_id(0) == 0)
    def _():
        s_ref[...] = jnp.zeros_like(s_ref)
        carry_ref[...] = jnp.zeros_like(carry_ref)

    row = _iota((LANES, LANES), 0)
    col = _iota((LANES, LANES), 1)
    strict = row > col
    incl = row >= col
    same_head = (row < HEAD_DIM) == (col < HEAD_DIM)
    bd_ones = jnp.where(same_head, 1.0, 0.0).astype(BF16)
    eye = jnp.where(row == col, 1.0, 0.0).astype(F32)
    tri = jnp.where(_iota((CHUNK, CHUNK), 0) >= _iota((CHUNK, CHUNK), 1), 1.0, 0.0).astype(BF16)
    lane_h0 = _iota((CHUNK, LANES), 1) < HEAD_DIM
    first_row = _iota((CHUNK, NA), 0) == 0
    wup = _split(wup_ref[...])
    aup = _split(aup_ref[...])

    def stack(a):
        return jnp.concatenate([jnp.where(lane_h0, a, 0.0), jnp.where(lane_h0, 0.0, a)], axis=0)

    def chunk_body(c, carry):
        r0 = pl.multiple_of(c * CHUNK, CHUNK)
        x = pa_ref[pl.ds(r0, CHUNK), :]
        prev = pltpu.roll(x, 1, 0)
        prev = jnp.where(first_row, carry_ref[7:8, :], prev)
        carry_ref[...] = x[CHUNK - 8:CHUNK, :]
        xs = x + mu_ref[...] * (prev - x)
        d = D_BRANCH
        r = xs[:, 0:d]
        k = xs[:, d:2 * d]
        v = xs[:, 2 * d:3 * d]
        g = xs[:, 3 * d:4 * d]
        wd = xs[:, 4 * d:4 * d + LANES]
        ad = xs[:, 4 * d + LANES:4 * d + 2 * LANES]
        lw = -DECAY_SCALE * _sigmoid(w0_ref[...] + _mm(_split(jnp.tanh(wd)), wup))
        a = _sigmoid(a0_ref[...] + _mm(_split(ad), aup))
        kkx = k * kk_ref[...]
        k2 = k * (1.0 + (a - 1.0) * ka_ref[...])
        bon = r * k2 * rk_ref[...]

        sls = [slice(p * LANES, (p + 1) * LANES) for p in range(N_PAIRS)]
        pre = []
        for sl in sls:
            rp, kp, lwp, kkxp = r[:, sl], k2[:, sl], lw[:, sl], kkx[:, sl]
            ss = _mm_exact_rhs(_split(kkxp * kkxp), bd_ones)
            kkp = kkxp / jnp.maximum(jnp.sqrt(ss), 1e-12)
            alpha = -kkp
            beta = kkp * a[:, sl]
            cl = _mm_exact_lhs(tri, _split(lwp))
            clp = cl - lwp
            cm = cl[CHUNK // 2 - 1:CHUNK // 2, :]
            cend = cl[CHUNK - 1:CHUNK, :]
            e_k = jnp.exp(cm - cl)
            e_h = jnp.exp(cend - cl)
            lhs = _split(jnp.concatenate([stack(alpha * jnp.exp(clp - cm)),
                                          stack(rp * jnp.exp(cl - cm))], axis=0))
            rhs = _split(jnp.concatenate([stack(beta * e_k), stack(kp * e_k)], axis=0))
            pre.append(dict(
                gram=_mm(lhs, rhs, _NT),
                a_e=_split(stack(alpha * jnp.exp(clp))),
                r_e=_split(stack(rp * jnp.exp(cl))),
                bk=_split(jnp.concatenate([stack(beta * e_h), stack(kp * e_h)], axis=0)),
                v_st=_split(stack(v[:, sl])),
                g_end=jnp.exp(cend)))
        n_ab = [jnp.where(strict, q["gram"][0:LANES, 0:LANES], 0.0) for q in pre]
        tinv = [eye + n for n in n_ab]
        npow = n_ab
        for _ in range(5):
            sps = [_split(n) for n in npow]
            npow = [_mm(sp, sp) for sp in sps]
            tinv = [t + _mm(_split(t), _split(n)) for t, n in zip(tinv, npow)]
        s0 = [s_ref[p] for p in range(N_PAIRS)]
        s0s = [_split(x0) for x0 in s0]
        u_rhs = [_mm(q["a_e"], ss0, _NT)
                 + _mm(_split(jnp.where(strict, q["gram"][0:LANES, LANES:2 * LANES], 0.0)), q["v_st"])
                 for q, ss0 in zip(pre, s0s)]
        u_sp = [_split(_mm(_split(t), _split(u))) for t, u in zip(tinv, u_rhs)]
        for p, (q, us) in enumerate(zip(pre, u_sp)):
            uv = (jnp.concatenate([us[0], q["v_st"][0]], axis=0),
                  jnp.concatenate([us[1], q["v_st"][1]], axis=0))
            s_ref[p] = s0[p] * q["g_end"] + _mm(uv, q["bk"], _TN)
        ys = []
        for q, ss0, us in zip(pre, s0s, u_sp):
            m_rb = jnp.where(incl, q["gram"][LANES:2 * LANES, 0:LANES], 0.0)
            m_rk = jnp.where(incl, q["gram"][LANES:2 * LANES, LANES:2 * LANES], 0.0)
            y_st = _mm(q["r_e"], ss0, _NT) + _mm(_split(m_rb), us) + _mm(_split(m_rk), q["v_st"])
            ys.append(y_st[0:CHUNK] + y_st[CHUNK:2 * CHUNK])
        bonus = [_mm_exact_rhs(_split(bon[:, sl]), bd_ones) * v[:, sl] for sl in sls]
        dlt = [y - _mm_exact_rhs(_split(y), bd_ones) * (1.0 / HEAD_DIM) for y in ys]
        var = [_mm_exact_rhs(_split(dl * dl), bd_ones) * (1.0 / HEAD_DIM) for dl in dlt]
        for sl, dl, vr, bo in zip(sls, dlt, var, bonus):
            yn = dl * lax.rsqrt(vr + GN_EPS) * gnw_ref[:, sl] + gnb_ref[:, sl]
            gp = g[:, sl]
            o_ref[pl.ds(r0, CHUNK), sl] = (yn + bo) * (gp * _sigmoid(gp))
        return carry

    lax.fori_loop(0, n_chunks, chunk_body, 0)


def _rwkv(pa, mu, w0, a0, k_k, k_a, r_k, gn_w, gn_b, w_up, a_up, tm):
    s = pa.shape[0]
    vec = lambda n: pl.BlockSpec((1, n), lambda i: (0, 0))
    lora = pl.BlockSpec((LANES, D_BRANCH), lambda i: (0, 0))
    return pl.pallas_call(
        functools.partial(_rwkv_kernel, n_chunks=tm // CHUNK),
        grid=(s // tm,),
        in_specs=[pl.BlockSpec((tm, NA), lambda i: (i, 0)), vec(NA)] + [vec(D_BRANCH)] * 7
                 + [lora, lora],
        out_specs=pl.BlockSpec((tm, D_BRANCH), lambda i: (i, 0)),
        out_shape=jax.ShapeDtypeStruct((s, D_BRANCH), F32),
        scratch_shapes=[pltpu.VMEM((N_PAIRS, LANES, LANES), F32),
                        pltpu.VMEM((8, NA), F32)],
        compiler_params=pltpu.CompilerParams(dimension_semantics=("arbitrary",),
                                             vmem_limit_bytes=VMEM_LIMIT),
        name="rwkv",
    )(pa, mu, w0, a0, k_k, k_a, r_k, gn_w, gn_b, w_up, a_up)


def _split_hi_lo(x, take_hi):
    hi = x.astype(BF16)
    lo = (x - hi.astype(F32)).astype(BF16)
    return jnp.where(take_hi, hi, lo)


def _dprep_kernel(pq_ref, pi_ref, qg_ref, kg_ref, qa_ref, qb_ref, k_ref, v_ref,
                  iq_ref, ik_ref, iw_ref):
    tm = pq_ref.shape[0]
    d = D_BRANCH
    row = _iota((LANES, LANES), 0)
    col = _iota((LANES, LANES), 1)
    bd_avg = jnp.where((row < HEAD_DIM) == (col < HEAD_DIM), 1.0 / HEAD_DIM, 0.0).astype(F32)
    lane_h0 = _iota((tm, LANES), 1) < HEAD_DIM
    q_scale = (HEAD_DIM ** -0.5) * LOG2E
    for p in range(N_PAIRS):
        sl = slice(p * LANES, (p + 1) * LANES)
        q = pq_ref[:, p * LANES:(p + 1) * LANES]
        k = pq_ref[:, d + p * LANES:d + (p + 1) * LANES]
        qn = q * lax.rsqrt(_hdot(q * q, bd_avg) + NORM_EPS) * qg_ref[:, sl]
        kn = k * lax.rsqrt(_hdot(k * k, bd_avg) + NORM_EPS) * kg_ref[:, sl]
        qs = (qn * q_scale).astype(BF16)
        qa_ref[:, sl] = jnp.where(lane_h0, qs, jnp.zeros_like(qs))
        qb_ref[:, sl] = jnp.where(lane_h0, jnp.zeros_like(qs), qs)
        k_ref[:, sl] = kn.astype(BF16)
    v_ref[...] = pq_ref[:, 2 * d:3 * d].astype(BF16)
    lane4 = _iota((tm, 4 * HEAD_DIM), 1) // HEAD_DIM
    q_hi = (lane4 % 2) == 0
    k_hi = lane4 < 2
    off = N_HEADS * 4 * HEAD_DIM
    iw = pi_ref[:, off + 4 * HEAD_DIM:off + 4 * HEAD_DIM + LANES] * (
        (N_HEADS ** -0.5) * (HEAD_DIM ** -0.5))
    for h in range(N_HEADS):
        sl = slice(h * 4 * HEAD_DIM, (h + 1) * 4 * HEAD_DIM)
        iq_ref[h] = _split_hi_lo(pi_ref[:, sl], q_hi)
        iw_ref[h] = jnp.broadcast_to(iw[:, h:h + 1], (tm, LANES))
    ik_ref[...] = _split_hi_lo(pi_ref[:, off:off + 4 * HEAD_DIM], k_hi)


def _dprep(pq, pi, qg, kg, tm):
    s = pq.shape[0]
    row = lambda n: pl.BlockSpec((tm, n), lambda i: (i, 0))
    hrow = lambda n: pl.BlockSpec((N_HEADS, tm, n), lambda i: (0, i, 0))
    vec = pl.BlockSpec((1, D_BRANCH), lambda i: (0, 0))
    kq = 4 * HEAD_DIM
    return pl.pallas_call(
        _dprep_kernel,
        grid=(s // tm,),
        in_specs=[row(pq.shape[1]), row(pi.shape[1]), vec, vec],
        out_specs=[row(D_BRANCH)] * 4 + [hrow(kq), row(kq), hrow(LANES)],
        out_shape=[jax.ShapeDtypeStruct((s, D_BRANCH), BF16)] * 4
                  + [jax.ShapeDtypeStruct((N_HEADS, s, kq), BF16),
                     jax.ShapeDtypeStruct((s, kq), BF16),
                     jax.ShapeDtypeStruct((N_HEADS, s, LANES), F32)],
        compiler_params=pltpu.CompilerParams(dimension_semantics=("arbitrary",),
                                             vmem_limit_bytes=VMEM_LIMIT),
        name="dprep",
    )(pq, pi, qg, kg)


def _block_tables(s):
    qb_l, kb_l, nkb_l = [], [], []
    for qb in range(s // Q_BLOCK):
        nkb = -(-(Q_BLOCK * (qb + 1)) // KEY_BLOCK)
        for kb in range(nkb):
            qb_l.append(qb)
            kb_l.append(kb)
            nkb_l.append(nkb)
    return (np.asarray(qb_l, np.int32), np.asarray(kb_l, np.int32), np.asarray(nkb_l, np.int32))


def _float_to_ckey(x):
    b = pltpu.bitcast(x, I32)
    k = b ^ ((b >> 31) & 0x7FFFFFFF)
    return jnp.where(k >= MIN_NORMAL_BITS, k - (MIN_NORMAL_BITS - 1),
                     jnp.where(k < -MIN_NORMAL_BITS, k + MIN_NORMAL_BITS, 0))


def _ckey_to_float(c):
    k = jnp.where(c > 0, c + (MIN_NORMAL_BITS - 1), jnp.where(c < 0, c - MIN_NORMAL_BITS, 0))
    return pltpu.bitcast(k ^ ((k >> 31) & 0x7FFFFFFF), F32)


def _select_kernel(qb_tab, kb_tab, nkb_tab, iq_ref, iw_ref, ik_ref, mask_ref, sc_ref, cnt_ref,
                   cand_ref, bst_ref, lok_ref, *, topk, seq):
    step = pl.program_id(0)
    qb = qb_tab[step]
    kb = kb_tab[step]
    nkb = nkb_tab[step]
    rows = Q_BLOCK

    lane_pos = _iota((ROW_SUB, KEY_SUB), 1)
    for rh in range(rows // ROW_SUB):
        rs = slice(rh * ROW_SUB, (rh + 1) * ROW_SUB)
        q_chunk = (qb * Q_BLOCK + rh * ROW_SUB + _iota((ROW_SUB, KEY_SUB), 0)) // CHUNK
        iq_all = iq_ref[:, rs, :].reshape(N_HEADS * ROW_SUB, 4 * HEAD_DIM)
        for kc in range(KEY_BLOCK // KEY_SUB):
            sc = _dot_nt(iq_all, ik_ref[kc * KEY_SUB:(kc + 1) * KEY_SUB, :])
            acc = jnp.zeros((ROW_SUB, KEY_SUB), F32)
            for h in range(N_HEADS):
                w = iw_ref[h, rs, :]
                acc = acc + jnp.maximum(sc[h * ROW_SUB:(h + 1) * ROW_SUB], 0.0) * jnp.concatenate(
                    [w] * (KEY_SUB // LANES), axis=1)
            base = kb * KEY_BLOCK + kc * KEY_SUB
            adm = ((base + lane_pos) // CHUNK) <= q_chunk
            sc_ref[rs, pl.ds(pl.multiple_of(base, KEY_SUB), KEY_SUB)] = jnp.where(adm, acc, -jnp.inf)

    @pl.when(kb == nkb - 1)
    def _():
        n_cnt = nkb * (KEY_BLOCK // COUNT_BLOCK)
        n_lb = COUNT_BLOCK // LANES

        def sweep(fn, init):
            outs = []
            for rh in range(rows // ROW_SUB):
                rs = slice(rh * ROW_SUB, (rh + 1) * ROW_SUB)

                def body(j, carry, rs=rs):
                    c0 = pl.multiple_of(j * COUNT_BLOCK, COUNT_BLOCK)
                    blk = sc_ref[rs, pl.ds(c0, COUNT_BLOCK)]
                    for i in range(n_lb):
                        carry = fn(carry, blk[:, i * LANES:(i + 1) * LANES], c0 + i * LANES, i % 2, rs)
                    return carry
                outs.append(lax.fori_loop(0, n_cnt, body, init))
            return jax.tree.map(lambda *xs: jnp.concatenate(xs, axis=0), *outs)

        def lane_reduce(x, red):
            return red(jnp.transpose(x), axis=0, keepdims=True)

        def row_bcast(x):
            return jnp.transpose(jnp.broadcast_to(x, (LANES, rows)))

        def count(pred):
            part = sweep(lambda part, blk, c0, par, rs: jnp.where(pred(blk, c0, rs), part + 1.0, part),
                         jnp.zeros((ROW_SUB, LANES), F32))
            return lane_reduce(part, jnp.sum)

        def count_ge(cand):
            cand_ref[...] = row_bcast(cand)
            return count(lambda blk, c0, rs: blk >= cand_ref[rs, :])

        neg = jnp.full((ROW_SUB, LANES), -jnp.inf, F32)
        g0, g1 = sweep(lambda g, blk, c0, par, rs: ((jnp.maximum(g[0], blk), g[1]) if par == 0
                                                     else (g[0], jnp.maximum(g[1], blk))), (neg, neg))
        row_max = lane_reduce(jnp.maximum(g0, g1), jnp.max)
        lo = lane_reduce(jnp.minimum(g0, g1), jnp.min)
        lo = _ckey_to_float(_float_to_ckey(jnp.maximum(lo - jnp.abs(lo) * BRACKET_SLACK, -FLT_MAX)))
        hi = _ckey_to_float(_float_to_ckey(row_max) + 1)
        cnt = count_ge(lo)
        short = jnp.where(cnt < topk, 1, 0)

        @pl.when(jnp.max(short) > 0)
        def _():
            cnt_ref[...] = count_ge(jnp.full((1, rows), -FLT_MAX, F32))

        @pl.when(jnp.max(short) == 0)
        def _():
            cnt_ref[...] = cnt

        lo = jnp.where(short > 0, -FLT_MAX, lo)
        cnt = jnp.where(short > 0, cnt_ref[...], cnt)
        act = jnp.where(cnt > topk, 1, 0)

        def cond(st):
            return st[6] > 0

        def body(st):
            lo, hi, cnt, chi, act, rnd, _ = st
            lk, hk = _float_to_ckey(lo), _float_to_ckey(hi)
            mid_k = _ckey_to_float((lk >> 1) + (hk >> 1) + (lk & hk & 1))
            mid_v = _ckey_to_float(_float_to_ckey(0.5 * lo + 0.5 * hi))
            early = jnp.where(rnd < VALUE_ROUNDS, 1, 0)
            use_v = early * jnp.where(mid_v > lo, 1, 0) * jnp.where(mid_v < hi, 1, 0)
            mid = jnp.where(use_v > 0, mid_v, mid_k)
            frac = jnp.log(cnt * (1.0 / topk)) / jnp.log(cnt / chi)
            frac = jnp.minimum(jnp.maximum(frac, PIVOT_CLIP), 1.0 - PIVOT_CLIP)
            mid_i = _ckey_to_float(_float_to_ckey(lo + (hi - lo) * frac))
            use_i = (early * jnp.where(cnt - chi > PIVOT_MIN_SPAN, 1, 0)
                     * jnp.where(mid_i > lo, 1, 0) * jnp.where(mid_i < hi, 1, 0))
            mid = jnp.where(use_i > 0, mid_i, mid)
            straddle = jnp.where(lo < 0.0, 1, 0) * jnp.where(hi > 0.0, 1, 0)
            at_zero = jnp.where(lo == 0.0, 1, 0) * jnp.where(hi > MIN_NORMAL, 1, 0)
            mid = jnp.where(straddle > 0, 0.0, jnp.where(at_zero > 0, MIN_NORMAL, mid))
            inside = act * jnp.where(mid > lo, 1, 0) * jnp.where(mid < hi, 1, 0)
            c = count_ge(mid)
            up = inside * jnp.where(c >= topk, 1, 0)
            dn = inside - up
            lo = jnp.where(up > 0, mid, lo)
            cnt = jnp.where(up > 0, c, cnt)
            hi = jnp.where(dn > 0, mid, hi)
            chi = jnp.where(dn > 0, jnp.maximum(c, 0.5), chi)
            act = inside * jnp.where(cnt > topk, 1, 0)
            return lo, hi, cnt, chi, act, rnd + 1, jnp.max(act)

        lo, hi, cnt, _, act, _, _ = lax.while_loop(
            cond, body, (lo, hi, cnt, jnp.full((1, rows), 0.5, F32), act, jnp.int32(0), jnp.max(act)))
        cand_ref[...] = row_bcast(lo)
        excess = jnp.max(cnt) > topk

        def write_mask(sel):
            for rh in range(rows // ROW_SUB):
                rs = slice(rh * ROW_SUB, (rh + 1) * ROW_SUB)

                def body(j, carry, rs=rs):
                    c0 = pl.multiple_of(j * COUNT_BLOCK, COUNT_BLOCK)
                    blk = sc_ref[rs, pl.ds(c0, COUNT_BLOCK)]
                    out = jnp.concatenate(
                        [jnp.where(sel(blk[:, i * LANES:(i + 1) * LANES], c0 + i * LANES, rs), 1.0, 0.0)
                         for i in range(n_lb)], axis=1)
                    mask_ref[rs, pl.ds(c0, COUNT_BLOCK)] = out.astype(jnp.int8)
                    return carry
                lax.fori_loop(0, n_cnt, body, 0)

        @pl.when(jnp.logical_not(excess))
        def _():
            write_mask(lambda blk, c0, rs: blk >= cand_ref[rs, :])

        @pl.when(excess)
        def _():
            n_gt = count(lambda blk, c0, rs: blk > cand_ref[rs, :])
            need = row_bcast(topk - n_gt)
            lane = _iota((ROW_SUB, LANES), 1)
            lane_all = _iota((rows, LANES), 1)

            def tie(blk, rs):
                return jnp.where(blk == cand_ref[rs, :], 1.0, 0.0)

            def block_of(c0):
                return lax.shift_right_logical(c0, LANES.bit_length() - 1)

            def running_sum(x):
                sh = 1
                while sh < LANES:
                    x = x + jnp.where(lane_all >= sh, pltpu.roll(x, sh, 1), 0.0)
                    sh *= 2
                return x

            ties = sweep(lambda t, blk, c0, par, rs: jnp.where(
                lane == block_of(c0), jnp.sum(tie(blk, rs), axis=1, keepdims=True), t),
                jnp.zeros((ROW_SUB, LANES), F32))
            whole = jnp.where(running_sum(ties) < need, 1.0, 0.0)
            bst_ref[...] = jnp.broadcast_to(jnp.sum(whole, axis=1, keepdims=True), (rows, LANES))
            left = need - jnp.sum(whole * ties, axis=1, keepdims=True)
            pat = sweep(lambda g, blk, c0, par, rs: jnp.where(
                bst_ref[rs, :] == block_of(c0).astype(F32), tie(blk, rs), g),
                jnp.zeros((ROW_SUB, LANES), F32))
            lok_ref[...] = jnp.where(running_sum(pat) <= left, 1.0, 0.0)

            def sel(blk, c0, rs):
                b = block_of(c0).astype(F32)
                part = (bst_ref[rs, :] == b) & (lok_ref[rs, :] > 0.0)
                return (blk > cand_ref[rs, :]) | ((blk == cand_ref[rs, :]) & ((bst_ref[rs, :] > b) | part))
            write_mask(sel)

        def zero_body(j, carry):
            c0 = pl.multiple_of(j * COUNT_BLOCK, COUNT_BLOCK)
            mask_ref[:, pl.ds(c0, COUNT_BLOCK)] = jnp.zeros((rows, COUNT_BLOCK), jnp.int8)
            return carry
        lax.fori_loop(n_cnt, seq // COUNT_BLOCK, zero_body, 0)


def _select(iq, iw, ik, topk):
    s = ik.shape[0]
    assert s // LANES <= LANES
    qb_t, kb_t, nkb_t = _block_tables(s)
    grid_spec = pltpu.PrefetchScalarGridSpec(
        num_scalar_prefetch=3,
        grid=(len(qb_t),),
        in_specs=[pl.BlockSpec((N_HEADS, Q_BLOCK, iq.shape[2]), lambda i, qb, kb, nk: (0, qb[i], 0)),
                  pl.BlockSpec((N_HEADS, Q_BLOCK, LANES), lambda i, qb, kb, nk: (0, qb[i], 0)),
                  pl.BlockSpec((KEY_BLOCK, ik.shape[1]), lambda i, qb, kb, nk: (kb[i], 0))],
        out_specs=pl.BlockSpec((Q_BLOCK, s), lambda i, qb, kb, nk: (qb[i], 0)),
        scratch_shapes=[pltpu.VMEM((Q_BLOCK, s + LANES), F32),
                        pltpu.VMEM((1, Q_BLOCK), F32),
                        pltpu.VMEM((Q_BLOCK, LANES), F32),
                        pltpu.VMEM((Q_BLOCK, LANES), F32),
                        pltpu.VMEM((Q_BLOCK, LANES), F32)],
    )
    return pl.pallas_call(
        functools.partial(_select_kernel, topk=topk, seq=s),
        grid_spec=grid_spec,
        out_shape=jax.ShapeDtypeStruct((s, s), jnp.int8),
        compiler_params=pltpu.CompilerParams(dimension_semantics=("arbitrary",),
                                             vmem_limit_bytes=VMEM_LIMIT),
        name="select",
    )(jnp.asarray(qb_t), jnp.asarray(kb_t), jnp.asarray(nkb_t), iq, iw, ik)


def _attn_kernel(qb_tab, kb_tab, nkb_tab, qa_ref, qb_ref, k_ref, v_ref, mask_ref, o_ref,
                 m_ref, acc_ref, bias_ref, s_ref):
    step = pl.program_id(0)
    kb = kb_tab[step]
    nkb = nkb_tab[step]
    n_sub = KEY_BLOCK // KEY_SUB

    @pl.when(kb == 0)
    def _():
        m_ref[...] = jnp.full(m_ref.shape, NEG_BIG, F32)
        acc_ref[...] = jnp.zeros_like(acc_ref)

    for kc in range(n_sub):
        ks = slice(kc * KEY_SUB, (kc + 1) * KEY_SUB)
        bias_ref[:, ks] = (mask_ref[:, ks].astype(F32) - 1.0) * (-NEG_BIG)

    ones = jnp.ones((KEY_SUB, LANES), BF16)
    q_refs = (qa_ref, qb_ref)
    lanes_of = lambda h: slice((h // 2) * LANES, (h // 2 + 1) * LANES)
    m_old, m_new = [], []
    for h in range(N_HEADS):
        sl = lanes_of(h)
        q = q_refs[h % 2][:, sl]
        bm = None
        for kc in range(n_sub):
            ks = slice(kc * KEY_SUB, (kc + 1) * KEY_SUB)
            s = _dot_nt(q, k_ref[ks, sl]) + bias_ref[:, ks]
            s_ref[h, :, ks] = s
            mx = jnp.max(s, axis=1, keepdims=True)
            bm = mx if bm is None else jnp.maximum(bm, mx)
        m_old.append(m_ref[h])
        m_new.append(jnp.maximum(m_old[h], bm))
    for h in range(N_HEADS):
        sl = lanes_of(h)
        m2 = jnp.concatenate([m_new[h]] * (KEY_SUB // LANES), axis=1)
        pv = None
        for kc in range(n_sub):
            ks = slice(kc * KEY_SUB, (kc + 1) * KEY_SUB)
            prob = jnp.exp2(s_ref[h, :, ks] - m2).astype(BF16)
            v_aug = jnp.concatenate([v_ref[ks, sl], ones], axis=1)
            d = jnp.dot(prob, v_aug, preferred_element_type=F32)
            pv = d if pv is None else pv + d
        alpha = jnp.exp2(m_old[h] - m_new[h])
        acc_ref[h] = acc_ref[h] * jnp.concatenate([alpha, alpha], axis=1) + pv
        m_ref[h] = m_new[h]

    @pl.when(kb == nkb - 1)
    def _():
        lane_h0 = _iota((Q_BLOCK, LANES), 1) < HEAD_DIM
        for p in range(N_PAIRS):
            sl = slice(p * LANES, (p + 1) * LANES)
            a0 = acc_ref[2 * p]
            a1 = acc_ref[2 * p + 1]
            o_ref[:, sl] = jnp.where(lane_h0, a0[:, 0:LANES] / a0[:, LANES:2 * LANES],
                                     a1[:, 0:LANES] / a1[:, LANES:2 * LANES])


def _attn(qa, qb, k, v, mask):
    s = k.shape[0]
    qb_t, kb_t, nkb_t = _block_tables(s)
    qspec = pl.BlockSpec((Q_BLOCK, D_BRANCH), lambda i, qb, kb, nk: (qb[i], 0))
    kspec = pl.BlockSpec((KEY_BLOCK, D_BRANCH), lambda i, qb, kb, nk: (kb[i], 0))
    grid_spec = pltpu.PrefetchScalarGridSpec(
        num_scalar_prefetch=3,
        grid=(len(qb_t),),
        in_specs=[qspec, qspec, kspec, kspec,
                  pl.BlockSpec((Q_BLOCK, KEY_BLOCK), lambda i, qb, kb, nk: (qb[i], kb[i]))],
        out_specs=qspec,
        scratch_shapes=[pltpu.VMEM((N_HEADS, Q_BLOCK, LANES), F32),
                        pltpu.VMEM((N_HEADS, Q_BLOCK, 2 * LANES), F32),
                        pltpu.VMEM((Q_BLOCK, KEY_BLOCK), F32),
                        pltpu.VMEM((N_HEADS, Q_BLOCK, KEY_BLOCK), F32)],
    )
    return pl.pallas_call(
        _attn_kernel,
        grid_spec=grid_spec,
        out_shape=jax.ShapeDtypeStruct((s, D_BRANCH), F32),
        compiler_params=pltpu.CompilerParams(dimension_semantics=("arbitrary",),
                                             vmem_limit_bytes=VMEM_LIMIT),
        name="attn",
    )(jnp.asarray(qb_t), jnp.asarray(kb_t), jnp.asarray(nkb_t), qa, qb, k, v, mask)


def _outproj_kernel(x_ref, za_ref, yb_ref, gb_ref, pg_ref, gate_ref, wa_ref, wb_ref, wo_ref,
                    o_ref):
    d = x_ref.shape[1]
    gb = gb_ref[...]
    zb = yb_ref[...] * (gb * _sigmoid(gb))
    ya = jnp.dot(za_ref[...].astype(BF16), wa_ref[...], preferred_element_type=F32)
    yb = jnp.dot(zb.astype(BF16), wb_ref[...], preferred_element_type=F32)
    merged = _sigmoid(pg_ref[:, 0:d]) * ya + _sigmoid(pg_ref[:, d:2 * d]) * yb
    out = jnp.dot(merged.astype(BF16), wo_ref[...], preferred_element_type=F32)
    o_ref[...] = x_ref[...] + gate_ref[...] * out


def _outproj(x2, za, yb, pq, pg, gate, wa, wb, wo, tm):
    s, d = x2.shape
    row = lambda n: pl.BlockSpec((tm, n), lambda i: (i, 0))
    wsp = lambda w: pl.BlockSpec(w.shape, lambda i: (0, 0))
    gb_col = 3 * D_BRANCH // D_BRANCH
    return pl.pallas_call(
        _outproj_kernel,
        grid=(s // tm,),
        in_specs=[row(d), row(D_BRANCH), row(D_BRANCH),
                  pl.BlockSpec((tm, D_BRANCH), lambda i: (i, gb_col)),
                  row(2 * d), pl.BlockSpec((1, d), lambda i: (0, 0)),
                  wsp(wa), wsp(wb), wsp(wo)],
        out_specs=row(d),
        out_shape=jax.ShapeDtypeStruct((s, d), F32),
        compiler_params=pltpu.CompilerParams(dimension_semantics=("arbitrary",),
                                             vmem_limit_bytes=VMEM_LIMIT),
        name="outproj",
    )(x2, za, yb, pq, pg, gate, wa, wb, wo)


def _pad_cols(w, n):
    return jnp.pad(w, ((0, 0), (0, n - w.shape[1])))


def _layer(x2, c, norm_w, w_ada, b_ada, w_in, mu, w0, w_up, a0, a_up, k_k, k_a, r_k, gn_w,
           gn_b, q_gain, k_gain, w_a_out, w_b_out, w_o):
    s, d = x2.shape
    db = D_BRANCH
    assert s % KEY_BLOCK == 0 and d % LANES == 0
    topk = min(TOPK_MAX, s // 4)

    mod = _ada(jnp.broadcast_to(c, (8, d)), w_ada, b_ada[None, :])[0:1]
    shift, scale, gate = mod[:, 0:d], mod[:, d:2 * d], mod[:, 2 * d:3 * d]

    n_rwkv = 4 * db + 2 * LORA
    def rwkv_cols(w):
        return jnp.concatenate(
            [w[:, 0:3 * db], w[:, 3 * db + 2 * LORA:n_rwkv],
             _pad_cols(w[:, 3 * db:3 * db + LORA], LANES),
             _pad_cols(w[:, 3 * db + LORA:3 * db + 2 * LORA], LANES)], axis=1)
    wa = rwkv_cols(w_in[:, 0:n_rwkv]).astype(BF16)
    mu_a = rwkv_cols(mu[None, :])
    o = n_rwkv
    wq = w_in[:, o:o + 4 * db].astype(BF16)
    o += 4 * db
    w_iq = w_in[:, o:o + N_HEADS * HEAD_DIM].reshape(d, N_HEADS, 1, HEAD_DIM)
    w_iq = jnp.broadcast_to(w_iq, (d, N_HEADS, 4, HEAD_DIM)).reshape(d, N_HEADS * 4 * HEAD_DIM)
    o += N_HEADS * HEAD_DIM
    w_ik = jnp.tile(w_in[:, o:o + HEAD_DIM], (1, 4))
    o += HEAD_DIM
    w_iw = _pad_cols(w_in[:, o:o + N_HEADS], LANES)
    o += N_HEADS
    wi = jnp.concatenate([w_iq, w_ik, w_iw], axis=1).astype(BF16)
    wg = w_in[:, o:o + 2 * d].astype(BF16)

    pa, pq, pi, pg = _inproj(x2, norm_w[None, :], scale, shift, wa, wq, wi, wg, tm=256)

    pad_rows = lambda w: jnp.pad(w, ((0, LANES - w.shape[0]), (0, 0)))
    za = _rwkv(pa, mu_a, w0[None, :], a0[None, :], k_k[None, :], k_a[None, :],
               r_k.reshape(1, db), gn_w[None, :], gn_b[None, :], pad_rows(w_up), pad_rows(a_up),
               tm=512)

    qg = jnp.tile(q_gain, N_HEADS)[None, :]
    kg = jnp.tile(k_gain, N_HEADS)[None, :]
    qa, qb, kn, vb, iq, ik, iw = _dprep(pq, pi, qg, kg, tm=512)
    mask = _select(iq, iw, ik, topk)
    yb = _attn(qa, qb, kn, vb, mask)

    return _outproj(x2, za, yb, pq, pg, gate, w_a_out.astype(BF16), w_b_out.astype(BF16),
                    w_o.astype(BF16), tm=512)


def kernel(x, c, norm_w, w_ada, b_ada, w_in, mu, w0, w_up, a0, a_up, k_k, k_a, r_k, gn_w, gn_b,
           q_gain, k_gain, w_a_out, w_b_out, w_o):
    b, s, d = x.shape
    outs = []
    for bi in range(b):
        xb = x[bi]
        for l in range(norm_w.shape[0]):
            xb = _layer(xb, c[bi:bi + 1], norm_w[l], w_ada[l], b_ada[l], w_in[l], mu[l], w0[l],
                        w_up[l], a0[l], a_up[l], k_k[l], k_a[l], r_k[l], gn_w[l], gn_b[l],
                        q_gain[l], k_gain[l], w_a_out[l], w_b_out[l], w_o[l])
        outs.append(xb)
    return jnp.stack(outs, axis=0)
```

```python
import functools

import numpy as np
import jax
import jax.numpy as jnp
from jax import lax
from jax.experimental import pallas as pl
from jax.experimental.pallas import tpu as pltpu

F32 = jnp.float32
BF16 = jnp.bfloat16
I32 = jnp.int32
HIGHEST = lax.Precision.HIGHEST

CHUNK = 64
HEAD_DIM = 64
N_HEADS = 8
D_BRANCH = N_HEADS * HEAD_DIM
LORA = 64
DECAY_SCALE = 0.606531
GN_EPS = 64e-5
NORM_EPS = 1e-6
TOPK_MAX = 256
Q_BLOCK = 256

LANES = 128
N_PAIRS = D_BRANCH // LANES
VMEM_LIMIT = 56 * 1024 * 1024

KEY_BLOCK = 1024
KEY_SUB = 256
ROW_SUB = 128
COUNT_BLOCK = 1024
T_COUNT_BLOCK = 512
T_COUNT_ACCS = 8
VALUE_ROUNDS = 20
BRACKET_SLACK = 2.0 ** -10
PIVOT_CLIP = 0.1
PIVOT_MIN_SPAN = 8.0
FLT_MAX = 3.4028234663852886e38
MIN_NORMAL_BITS = 0x00800000
MIN_NORMAL = 1.1754943508222875e-38
NEG_BIG = -1e30
LOG2E = 1.4426950408889634

NA = 4 * D_BRANCH + 2 * LANES
NI = N_HEADS * 4 * HEAD_DIM + 4 * HEAD_DIM + LANES


def _sigmoid(x):
    return 1.0 / (1.0 + jnp.exp(-x))


def _hdot(a, b):
    return jnp.dot(a, b, precision=HIGHEST, preferred_element_type=F32)


def _split(x):
    hi = x.astype(BF16)
    lo = (x - hi.astype(F32)).astype(BF16)
    return hi, lo


def _dot(a, b, dims):
    return lax.dot_general(a, b, (dims, ((), ())), preferred_element_type=F32)


_NN = ((1,), (0,))
_NT = ((1,), (1,))
_TN = ((0,), (0,))


def _mm(sa, sb, dims=_NN):
    (ah, al), (bh, bl) = sa, sb
    ca, cb = dims[0][0], dims[1][0]
    return (_dot(jnp.concatenate([ah, al], axis=ca), jnp.concatenate([bh, bh], axis=cb), dims)
            + _dot(ah, bl, dims))


def _mm_exact_rhs(sa, b, dims=_NN):
    ah, al = sa
    ca, cb = dims[0][0], dims[1][0]
    return _dot(jnp.concatenate([ah, al], axis=ca), jnp.concatenate([b, b], axis=cb), dims)


def _mm_exact_lhs(a, sb, dims=_NN):
    bh, bl = sb
    ca, cb = dims[0][0], dims[1][0]
    return _dot(jnp.concatenate([a, a], axis=ca), jnp.concatenate([bh, bl], axis=cb), dims)


def _dot_nt(a, b):
    return lax.dot_general(a, b, (((1,), (1,)), ((), ())), preferred_element_type=F32)


def _iota(shape, dim):
    return lax.broadcasted_iota(I32, shape, dim)


def _ada_kernel(c_ref, w_ref, b_ref, o_ref):
    c = c_ref[...]
    s = c * _sigmoid(c)
    o_ref[...] = _hdot(s, w_ref[...]) + b_ref[...]


def _ada(c8, w_ada, b_ada):
    d = c8.shape[1]
    n = w_ada.shape[1]
    return pl.pallas_call(
        _ada_kernel,
        grid=(n // d,),
        in_specs=[pl.BlockSpec((8, d), lambda j: (0, 0)),
                  pl.BlockSpec((d, d), lambda j: (0, j)),
                  pl.BlockSpec((1, d), lambda j: (0, j))],
        out_specs=pl.BlockSpec((8, d), lambda j: (0, j)),
        out_shape=jax.ShapeDtypeStruct((8, n), F32),
        name="ada",
    )(c8, w_ada, b_ada)


def _inproj_kernel(x_ref, nw_ref, sc_ref, sh_ref, wa_ref, wq_ref, wi_ref, wg_ref,
                   pa_ref, pq_ref, pi_ref, pg_ref):
    x = x_ref[...]
    ms = jnp.mean(x * x, axis=-1, keepdims=True)
    h = x * lax.rsqrt(ms + NORM_EPS) * nw_ref[...]
    h = h * (1.0 + sc_ref[...]) + sh_ref[...]
    hb = h.astype(BF16)
    pa_ref[...] = jnp.dot(hb, wa_ref[...], preferred_element_type=F32)
    pq_ref[...] = jnp.dot(hb, wq_ref[...], preferred_element_type=F32)
    pi_ref[...] = jnp.dot(hb, wi_ref[...], preferred_element_type=F32)
    pg_ref[...] = jnp.dot(hb, wg_ref[...], preferred_element_type=F32)


def _inproj(x2, nw, scale, shift, wa, wq, wi, wg, tm):
    s, d = x2.shape
    row = lambda n: pl.BlockSpec((tm, n), lambda i: (i, 0))
    vec = pl.BlockSpec((1, d), lambda i: (0, 0))
    wsp = lambda w: pl.BlockSpec(w.shape, lambda i: (0, 0))
    return pl.pallas_call(
        _inproj_kernel,
        grid=(s // tm,),
        in_specs=[row(d), vec, vec, vec, wsp(wa), wsp(wq), wsp(wi), wsp(wg)],
        out_specs=[row(wa.shape[1]), row(wq.shape[1]), row(wi.shape[1]), row(wg.shape[1])],
        out_shape=[jax.ShapeDtypeStruct((s, w.shape[1]), F32) for w in (wa, wq, wi, wg)],
        compiler_params=pltpu.CompilerParams(dimension_semantics=("arbitrary",),
                                             vmem_limit_bytes=VMEM_LIMIT),
        name="inproj",
    )(x2, nw, scale, shift, wa, wq, wi, wg)


def _rwkv_kernel(pa_ref, mu_ref, w0_ref, a0_ref, kk_ref, ka_ref, rk_ref, gnw_ref, gnb_ref,
                 wup_ref, aup_ref, o_ref, s_ref, carry_ref, *, n_chunks):
    @pl.when(pl.program_id(0) == 0)
    def _():
        s_ref[...] = jnp.zeros_like(s_ref)
        carry_ref[...] = jnp.zeros_like(carry_ref)

    row = _iota((LANES, LANES), 0)
    col = _iota((LANES, LANES), 1)
    strict = row > col
    incl = row >= col
    same_head = (row < HEAD_DIM) == (col < HEAD_DIM)
    bd_ones = jnp.where(same_head, 1.0, 0.0).astype(BF16)
    eye = jnp.where(row == col, 1.0, 0.0).astype(F32)
    tri = jnp.where(_iota((CHUNK, CHUNK), 0) >= _iota((CHUNK, CHUNK), 1), 1.0, 0.0).astype(BF16)
    lane_h0 = _iota((CHUNK, LANES), 1) < HEAD_DIM
    first_row = _iota((CHUNK, NA), 0) == 0
    wup = _split(wup_ref[...])
    aup = _split(aup_ref[...])

    def stack(a):
        return jnp.concatenate([jnp.where(lane_h0, a, 0.0), jnp.where(lane_h0, 0.0, a)], axis=0)

    def chunk_body(c, carry):
        r0 = pl.multiple_of(c * CHUNK, CHUNK)
        x = pa_ref[pl.ds(r0, CHUNK), :]
        prev = pltpu.roll(x, 1, 0)
        prev = jnp.where(first_row, carry_ref[7:8, :], prev)
        carry_ref[...] = x[CHUNK - 8:CHUNK, :]
        xs = x + mu_ref[...] * (prev - x)
        d = D_BRANCH
        r = xs[:, 0:d]
        k = xs[:, d:2 * d]
        v = xs[:, 2 * d:3 * d]
        g = xs[:, 3 * d:4 * d]
        wd = xs[:, 4 * d:4 * d + LANES]
        ad = xs[:, 4 * d + LANES:4 * d + 2 * LANES]
        lw = -DECAY_SCALE * _sigmoid(w0_ref[...] + _mm(_split(jnp.tanh(wd)), wup))
        a = _sigmoid(a0_ref[...] + _mm(_split(ad), aup))
        kkx = k * kk_ref[...]
        k2 = k * (1.0 + (a - 1.0) * ka_ref[...])
        bon = r * k2 * rk_ref[...]

        sls = [slice(p * LANES, (p + 1) * LANES) for p in range(N_PAIRS)]
        pre = []
        for sl in sls:
            rp, kp, lwp, kkxp = r[:, sl], k2[:, sl], lw[:, sl], kkx[:, sl]
            ss = _mm_exact_rhs(_split(kkxp * kkxp), bd_ones)
            kkp = kkxp / jnp.maximum(jnp.sqrt(ss), 1e-12)
            alpha = -kkp
            beta = kkp * a[:, sl]
            cl = _mm_exact_lhs(tri, _split(lwp))
            clp = cl - lwp
            cm = cl[CHUNK // 2 - 1:CHUNK // 2, :]
            cend = cl[CHUNK - 1:CHUNK, :]
            e_k = jnp.exp(cm - cl)
            e_h = jnp.exp(cend - cl)
            lhs = _split(jnp.concatenate([stack(alpha * jnp.exp(clp - cm)),
                                          stack(rp * jnp.exp(cl - cm))], axis=0))
            rhs = _split(jnp.concatenate([stack(beta * e_k), stack(kp * e_k)], axis=0))
            pre.append(dict(
                gram=_mm(lhs, rhs, _NT),
                a_e=_split(stack(alpha * jnp.exp(clp))),
                r_e=_split(stack(rp * jnp.exp(cl))),
                bk=_split(jnp.concatenate([stack(beta * e_h), stack(kp * e_h)], axis=0)),
                v_st=_split(stack(v[:, sl])),
                g_end=jnp.exp(cend)))
        n_ab = [jnp.where(strict, q["gram"][0:LANES, 0:LANES], 0.0) for q in pre]
        tinv = [eye + n for n in n_ab]
        npow = n_ab
        for _ in range(5):
            sps = [_split(n) for n in npow]
            npow = [_mm(sp, sp) for sp in sps]
            tinv = [t + _mm(_split(t), _split(n)) for t, n in zip(tinv, npow)]
        s0 = [s_ref[p] for p in range(N_PAIRS)]
        s0s = [_split(x0) for x0 in s0]
        u_rhs = [_mm(q["a_e"], ss0, _NT)
                 + _mm(_split(jnp.where(strict, q["gram"][0:LANES, LANES:2 * LANES], 0.0)), q["v_st"])
                 for q, ss0 in zip(pre, s0s)]
        u_sp = [_split(_mm(_split(t), _split(u))) for t, u in zip(tinv, u_rhs)]
        for p, (q, us) in enumerate(zip(pre, u_sp)):
            uv = (jnp.concatenate([us[0], q["v_st"][0]], axis=0),
                  jnp.concatenate([us[1], q["v_st"][1]], axis=0))
            s_ref[p] = s0[p] * q["g_end"] + _mm(uv, q["bk"], _TN)
        ys = []
        for q, ss0, us in zip(pre, s0s, u_sp):
            m_rb = jnp.where(incl, q["gram"][LANES:2 * LANES, 0:LANES], 0.0)
            m_rk = jnp.where(incl, q["gram"][LANES:2 * LANES, LANES:2 * LANES], 0.0)
            y_st = _mm(q["r_e"], ss0, _NT) + _mm(_split(m_rb), us) + _mm(_split(m_rk), q["v_st"])
            ys.append(y_st[0:CHUNK] + y_st[CHUNK:2 * CHUNK])
        bonus = [_mm_exact_rhs(_split(bon[:, sl]), bd_ones) * v[:, sl] for sl in sls]
        dlt = [y - _mm_exact_rhs(_split(y), bd_ones) * (1.0 / HEAD_DIM) for y in ys]
        var = [_mm_exact_rhs(_split(dl * dl), bd_ones) * (1.0 / HEAD_DIM) for dl in dlt]
        for sl, dl, vr, bo in zip(sls, dlt, var, bonus):
            yn = dl * lax.rsqrt(vr + GN_EPS) * gnw_ref[:, sl] + gnb_ref[:, sl]
            gp = g[:, sl]
            o_ref[pl.ds(r0, CHUNK), sl] = (yn + bo) * (gp * _sigmoid(gp))
        return carry

    lax.fori_loop(0, n_chunks, chunk_body, 0)


def _rwkv(pa, mu, w0, a0, k_k, k_a, r_k, gn_w, gn_b, w_up, a_up, tm):
    s = pa.shape[0]
    vec = lambda n: pl.BlockSpec((1, n), lambda i: (0, 0))
    lora = pl.BlockSpec((LANES, D_BRANCH), lambda i: (0, 0))
    return pl.pallas_call(
        functools.partial(_rwkv_kernel, n_chunks=tm // CHUNK),
        grid=(s // tm,),
        in_specs=[pl.BlockSpec((tm, NA), lambda i: (i, 0)), vec(NA)] + [vec(D_BRANCH)] * 7
                 + [lora, lora],
        out_specs=pl.BlockSpec((tm, D_BRANCH), lambda i: (i, 0)),
        out_shape=jax.ShapeDtypeStruct((s, D_BRANCH), F32),
        scratch_shapes=[pltpu.VMEM((N_PAIRS, LANES, LANES), F32),
                        pltpu.VMEM((8, NA), F32)],
        compiler_params=pltpu.CompilerParams(dimension_semantics=("arbitrary",),
                                             vmem_limit_bytes=VMEM_LIMIT),
        name="rwkv",
    )(pa, mu, w0, a0, k_k, k_a, r_k, gn_w, gn_b, w_up, a_up)


def _split_hi_lo(x, take_hi):
    hi = x.astype(BF16)
    lo = (x - hi.astype(F32)).astype(BF16)
    return jnp.where(take_hi, hi, lo)


def _dprep_kernel(pq_ref, pi_ref, qg_ref, kg_ref, qa_ref, qb_ref, k_ref, v_ref,
                  iq_ref, ik_ref, iw_ref):
    tm = pq_ref.shape[0]
    d = D_BRANCH
    row = _iota((LANES, LANES), 0)
    col = _iota((LANES, LANES), 1)
    bd_avg = jnp.where((row < HEAD_DIM) == (col < HEAD_DIM), 1.0 / HEAD_DIM, 0.0).astype(F32)
    lane_h0 = _iota((tm, LANES), 1) < HEAD_DIM
    q_scale = (HEAD_DIM ** -0.5) * LOG2E
    for p in range(N_PAIRS):
        sl = slice(p * LANES, (p + 1) * LANES)
        q = pq_ref[:, p * LANES:(p + 1) * LANES]
        k = pq_ref[:, d + p * LANES:d + (p + 1) * LANES]
        qn = q * lax.rsqrt(_hdot(q * q, bd_avg) + NORM_EPS) * qg_ref[:, sl]
        kn = k * lax.rsqrt(_hdot(k * k, bd_avg) + NORM_EPS) * kg_ref[:, sl]
        qs = (qn * q_scale).astype(BF16)
        qa_ref[:, sl] = jnp.where(lane_h0, qs, jnp.zeros_like(qs))
        qb_ref[:, sl] = jnp.where(lane_h0, jnp.zeros_like(qs), qs)
        k_ref[:, sl] = kn.astype(BF16)
    v_ref[...] = pq_ref[:, 2 * d:3 * d].astype(BF16)
    lane4 = _iota((tm, 4 * HEAD_DIM), 1) // HEAD_DIM
    q_hi = (lane4 % 2) == 0
    k_hi = lane4 < 2
    off = N_HEADS * 4 * HEAD_DIM
    iw = pi_ref[:, off + 4 * HEAD_DIM:off + 4 * HEAD_DIM + LANES] * (
        (N_HEADS ** -0.5) * (HEAD_DIM ** -0.5))
    for h in range(N_HEADS):
        sl = slice(h * 4 * HEAD_DIM, (h + 1) * 4 * HEAD_DIM)
        iq_ref[h] = _split_hi_lo(pi_ref[:, sl], q_hi)
        iw_ref[h] = jnp.broadcast_to(iw[:, h:h + 1], (tm, LANES))
    ik_ref[...] = _split_hi_lo(pi_ref[:, off:off + 4 * HEAD_DIM], k_hi)


def _dprep(pq, pi, qg, kg, tm):
    s = pq.shape[0]
    row = lambda n: pl.BlockSpec((tm, n), lambda i: (i, 0))
    hrow = lambda n: pl.BlockSpec((N_HEADS, tm, n), lambda i: (0, i, 0))
    vec = pl.BlockSpec((1, D_BRANCH), lambda i: (0, 0))
    kq = 4 * HEAD_DIM
    return pl.pallas_call(
        _dprep_kernel,
        grid=(s // tm,),
        in_specs=[row(pq.shape[1]), row(pi.shape[1]), vec, vec],
        out_specs=[row(D_BRANCH)] * 4 + [hrow(kq), row(kq), hrow(LANES)],
        out_shape=[jax.ShapeDtypeStruct((s, D_BRANCH), BF16)] * 4
                  + [jax.ShapeDtypeStruct((N_HEADS, s, kq), BF16),
                     jax.ShapeDtypeStruct((s, kq), BF16),
                     jax.ShapeDtypeStruct((N_HEADS, s, LANES), F32)],
        compiler_params=pltpu.CompilerParams(dimension_semantics=("arbitrary",),
                                             vmem_limit_bytes=VMEM_LIMIT),
        name="dprep",
    )(pq, pi, qg, kg)


def _block_tables(s):
    qb_l, kb_l, nkb_l = [], [], []
    for qb in range(s // Q_BLOCK):
        nkb = -(-(Q_BLOCK * (qb + 1)) // KEY_BLOCK)
        for kb in range(nkb):
            qb_l.append(qb)
            kb_l.append(kb)
            nkb_l.append(nkb)
    return (np.asarray(qb_l, np.int32), np.asarray(kb_l, np.int32), np.asarray(nkb_l, np.int32))


def _float_to_ckey(x):
    b = pltpu.bitcast(x, I32)
    k = b ^ ((b >> 31) & 0x7FFFFFFF)
    return jnp.where(k >= MIN_NORMAL_BITS, k - (MIN_NORMAL_BITS - 1),
                     jnp.where(k < -MIN_NORMAL_BITS, k + MIN_NORMAL_BITS, 0))


def _ckey_to_float(c):
    k = jnp.where(c > 0, c + (MIN_NORMAL_BITS - 1), jnp.where(c < 0, c - MIN_NORMAL_BITS, 0))
    return pltpu.bitcast(k ^ ((k >> 31) & 0x7FFFFFFF), F32)


def _select_kernel(qb_tab, kb_tab, nkb_tab, iq_ref, iw_ref, ik_ref, mask_ref, sc_ref, cnt_ref,
                   cand_ref, bst_ref, lok_ref, sct_ref, *, topk, seq):
    step = pl.program_id(0)
    qb = qb_tab[step]
    kb = kb_tab[step]
    nkb = nkb_tab[step]
    rows = Q_BLOCK

    lane_pos = _iota((ROW_SUB, KEY_SUB), 1)
    for rh in range(rows // ROW_SUB):
        rs = slice(rh * ROW_SUB, (rh + 1) * ROW_SUB)
        q_chunk = (qb * Q_BLOCK + rh * ROW_SUB + _iota((ROW_SUB, KEY_SUB), 0)) // CHUNK
        iq_all = iq_ref[:, rs, :].reshape(N_HEADS * ROW_SUB, 4 * HEAD_DIM)
        for kc in range(KEY_BLOCK // KEY_SUB):
            sc = _dot_nt(iq_all, ik_ref[kc * KEY_SUB:(kc + 1) * KEY_SUB, :])
            acc = jnp.zeros((ROW_SUB, KEY_SUB), F32)
            for h in range(N_HEADS):
                w = iw_ref[h, rs, :]
                acc = acc + jnp.maximum(sc[h * ROW_SUB:(h + 1) * ROW_SUB], 0.0) * jnp.concatenate(
                    [w] * (KEY_SUB // LANES), axis=1)
            base = kb * KEY_BLOCK + kc * KEY_SUB
            adm = ((base + lane_pos) // CHUNK) <= q_chunk
            scores = jnp.where(adm, acc, -jnp.inf)
            sc_ref[rs, pl.ds(pl.multiple_of(base, KEY_SUB), KEY_SUB)] = scores
            sct_ref[pl.ds(pl.multiple_of(base, KEY_SUB), KEY_SUB), rs] = jnp.transpose(scores)

    @pl.when(kb == nkb - 1)
    def _():
        n_cnt = nkb * (KEY_BLOCK // COUNT_BLOCK)
        n_lb = COUNT_BLOCK // LANES

        def sweep(fn, init):
            outs = []
            for rh in range(rows // ROW_SUB):
                rs = slice(rh * ROW_SUB, (rh + 1) * ROW_SUB)

                def body(j, carry, rs=rs):
                    c0 = pl.multiple_of(j * COUNT_BLOCK, COUNT_BLOCK)
                    blk = sc_ref[rs, pl.ds(c0, COUNT_BLOCK)]
                    for i in range(n_lb):
                        carry = fn(carry, blk[:, i * LANES:(i + 1) * LANES], c0 + i * LANES, i % 2, rs)
                    return carry
                outs.append(lax.fori_loop(0, n_cnt, body, init))
            return jax.tree.map(lambda *xs: jnp.concatenate(xs, axis=0), *outs)

        def lane_reduce(x, red):
            return red(jnp.transpose(x), axis=0, keepdims=True)

        def row_bcast(x):
            return jnp.transpose(jnp.broadcast_to(x, (LANES, rows)))

        def count(pred):
            part = sweep(lambda part, blk, c0, par, rs: jnp.where(pred(blk, c0, rs), part + 1.0, part),
                         jnp.zeros((ROW_SUB, LANES), F32))
            return lane_reduce(part, jnp.sum)

        def count_ge(cand):
            cand_ref[...] = row_bcast(cand)
            return count(lambda blk, c0, rs: blk >= cand_ref[rs, :])

        def count_ge_t(cand):
            cb = jnp.broadcast_to(cand, (8, rows))

            def body(j, parts):
                c0 = pl.multiple_of(j * T_COUNT_BLOCK, T_COUNT_BLOCK)
                x = sct_ref[pl.ds(c0, T_COUNT_BLOCK), :]
                parts = list(parts)
                for i in range(T_COUNT_BLOCK // 8):
                    a = i % T_COUNT_ACCS
                    parts[a] = jnp.where(x[i * 8:(i + 1) * 8, :] >= cb, parts[a] + 1.0, parts[a])
                return tuple(parts)
            parts = lax.fori_loop(0, nkb * (KEY_BLOCK // T_COUNT_BLOCK), body,
                                  tuple(jnp.zeros((8, rows), F32) for _ in range(T_COUNT_ACCS)))
            return jnp.sum(functools.reduce(lambda u, v: u + v, parts), axis=0, keepdims=True)

        neg = jnp.full((ROW_SUB, LANES), -jnp.inf, F32)
        g0, g1 = sweep(lambda g, blk, c0, par, rs: ((jnp.maximum(g[0], blk), g[1]) if par == 0
                                                     else (g[0], jnp.maximum(g[1], blk))), (neg, neg))
        row_max = lane_reduce(jnp.maximum(g0, g1), jnp.max)
        lo = lane_reduce(jnp.minimum(g0, g1), jnp.min)
        lo = _ckey_to_float(_float_to_ckey(jnp.maximum(lo - jnp.abs(lo) * BRACKET_SLACK, -FLT_MAX)))
        hi = _ckey_to_float(_float_to_ckey(row_max) + 1)
        cnt = count_ge(lo)
        short = jnp.where(cnt < topk, 1, 0)

        @pl.when(jnp.max(short) > 0)
        def _():
            cnt_ref[...] = count_ge(jnp.full((1, rows), -FLT_MAX, F32))

        @pl.when(jnp.max(short) == 0)
        def _():
            cnt_ref[...] = cnt

        lo = jnp.where(short > 0, -FLT_MAX, lo)
        cnt = jnp.where(short > 0, cnt_ref[...], cnt)
        act = jnp.where(cnt > topk, 1, 0)

        def cond(st):
            return st[6] > 0

        def body(st):
            lo, hi, cnt, chi, act, rnd, _ = st
            lk, hk = _float_to_ckey(lo), _float_to_ckey(hi)
            mid_k = _ckey_to_float((lk >> 1) + (hk >> 1) + (lk & hk & 1))
            mid_v = _ckey_to_float(_float_to_ckey(0.5 * lo + 0.5 * hi))
            early = jnp.where(rnd < VALUE_ROUNDS, 1, 0)
            use_v = early * jnp.where(mid_v > lo, 1, 0) * jnp.where(mid_v < hi, 1, 0)
            mid = jnp.where(use_v > 0, mid_v, mid_k)
            frac = jnp.log(cnt * (1.0 / topk)) / jnp.log(cnt / chi)
            frac = jnp.minimum(jnp.maximum(frac, PIVOT_CLIP), 1.0 - PIVOT_CLIP)
            mid_i = _ckey_to_float(_float_to_ckey(lo + (hi - lo) * frac))
            use_i = (early * jnp.where(cnt - chi > PIVOT_MIN_SPAN, 1, 0)
                     * jnp.where(mid_i > lo, 1, 0) * jnp.where(mid_i < hi, 1, 0))
            mid = jnp.where(use_i > 0, mid_i, mid)
            straddle = jnp.where(lo < 0.0, 1, 0) * jnp.where(hi > 0.0, 1, 0)
            at_zero = jnp.where(lo == 0.0, 1, 0) * jnp.where(hi > MIN_NORMAL, 1, 0)
            mid = jnp.where(straddle > 0, 0.0, jnp.where(at_zero > 0, MIN_NORMAL, mid))
            inside = act * jnp.where(mid > lo, 1, 0) * jnp.where(mid < hi, 1, 0)
            c = count_ge_t(mid)
            up = inside * jnp.where(c >= topk, 1, 0)
            dn = inside - up
            lo = jnp.where(up > 0, mid, lo)
            cnt = jnp.where(up > 0, c, cnt)
            hi = jnp.where(dn > 0, mid, hi)
            chi = jnp.where(dn > 0, jnp.maximum(c, 0.5), chi)
            act = inside * jnp.where(cnt > topk, 1, 0)
            return lo, hi, cnt, chi, act, rnd + 1, jnp.max(act)

        lo, hi, cnt, _, act, _, _ = lax.while_loop(
            cond, body, (lo, hi, cnt, jnp.full((1, rows), 0.5, F32), act, jnp.int32(0), jnp.max(act)))
        cand_ref[...] = row_bcast(lo)
        excess = jnp.max(cnt) > topk

        def write_mask(sel):
            for rh in range(rows // ROW_SUB):
                rs = slice(rh * ROW_SUB, (rh + 1) * ROW_SUB)

                def body(j, carry, rs=rs):
                    c0 = pl.multiple_of(j * COUNT_BLOCK, COUNT_BLOCK)
                    blk = sc_ref[rs, pl.ds(c0, COUNT_BLOCK)]
                    out = jnp.concatenate(
                        [jnp.where(sel(blk[:, i * LANES:(i + 1) * LANES], c0 + i * LANES, rs), 1.0, 0.0)
                         for i in range(n_lb)], axis=1)
                    mask_ref[rs, pl.ds(c0, COUNT_BLOCK)] = out.astype(jnp.int8)
                    return carry
                lax.fori_loop(0, n_cnt, body, 0)

        @pl.when(jnp.logical_not(excess))
        def _():
            write_mask(lambda blk, c0, rs: blk >= cand_ref[rs, :])

        @pl.when(excess)
        def _():
            n_gt = count(lambda blk, c0, rs: blk > cand_ref[rs, :])
            need = row_bcast(topk - n_gt)
            lane = _iota((ROW_SUB, LANES), 1)
            lane_all = _iota((rows, LANES), 1)

            def tie(blk, rs):
                return jnp.where(blk == cand_ref[rs, :], 1.0, 0.0)

            def block_of(c0):
                return lax.shift_right_logical(c0, LANES.bit_length() - 1)

            def running_sum(x):
                sh = 1
                while sh < LANES:
                    x = x + jnp.where(lane_all >= sh, pltpu.roll(x, sh, 1), 0.0)
                    sh *= 2
                return x

            ties = sweep(lambda t, blk, c0, par, rs: jnp.where(
                lane == block_of(c0), jnp.sum(tie(blk, rs), axis=1, keepdims=True), t),
                jnp.zeros((ROW_SUB, LANES), F32))
            whole = jnp.where(running_sum(ties) < need, 1.0, 0.0)
            bst_ref[...] = jnp.broadcast_to(jnp.sum(whole, axis=1, keepdims=True), (rows, LANES))
            left = need - jnp.sum(whole * ties, axis=1, keepdims=True)
            pat = sweep(lambda g, blk, c0, par, rs: jnp.where(
                bst_ref[rs, :] == block_of(c0).astype(F32), tie(blk, rs), g),
                jnp.zeros((ROW_SUB, LANES), F32))
            lok_ref[...] = jnp.where(running_sum(pat) <= left, 1.0, 0.0)

            def sel(blk, c0, rs):
                b = block_of(c0).astype(F32)
                part = (bst_ref[rs, :] == b) & (lok_ref[rs, :] > 0.0)
                return (blk > cand_ref[rs, :]) | ((blk == cand_ref[rs, :]) & ((bst_ref[rs, :] > b) | part))
            write_mask(sel)

        def zero_body(j, carry):
            c0 = pl.multiple_of(j * COUNT_BLOCK, COUNT_BLOCK)
            mask_ref[:, pl.ds(c0, COUNT_BLOCK)] = jnp.zeros((rows, COUNT_BLOCK), jnp.int8)
            return carry
        lax.fori_loop(n_cnt, seq // COUNT_BLOCK, zero_body, 0)


def _select(iq, iw, ik, topk):
    s = ik.shape[0]
    assert s // LANES <= LANES
    qb_t, kb_t, nkb_t = _block_tables(s)
    grid_spec = pltpu.PrefetchScalarGridSpec(
        num_scalar_prefetch=3,
        grid=(len(qb_t),),
        in_specs=[pl.BlockSpec((N_HEADS, Q_BLOCK, iq.shape[2]), lambda i, qb, kb, nk: (0, qb[i], 0)),
                  pl.BlockSpec((N_HEADS, Q_BLOCK, LANES), lambda i, qb, kb, nk: (0, qb[i], 0)),
                  pl.BlockSpec((KEY_BLOCK, ik.shape[1]), lambda i, qb, kb, nk: (kb[i], 0))],
        out_specs=pl.BlockSpec((Q_BLOCK, s), lambda i, qb, kb, nk: (qb[i], 0)),
        scratch_shapes=[pltpu.VMEM((Q_BLOCK, s + LANES), F32),
                        pltpu.VMEM((1, Q_BLOCK), F32),
                        pltpu.VMEM((Q_BLOCK, LANES), F32),
                        pltpu.VMEM((Q_BLOCK, LANES), F32),
                        pltpu.VMEM((Q_BLOCK, LANES), F32),
                        pltpu.VMEM((s, Q_BLOCK), F32)],
    )
    return pl.pallas_call(
        functools.partial(_select_kernel, topk=topk, seq=s),
        grid_spec=grid_spec,
        out_shape=jax.ShapeDtypeStruct((s, s), jnp.int8),
        compiler_params=pltpu.CompilerParams(dimension_semantics=("arbitrary",),
                                             vmem_limit_bytes=VMEM_LIMIT),
        name="select",
    )(jnp.asarray(qb_t), jnp.asarray(kb_t), jnp.asarray(nkb_t), iq, iw, ik)


def _attn_kernel(qb_tab, kb_tab, nkb_tab, qa_ref, qb_ref, k_ref, v_ref, mask_ref, o_ref,
                 m_ref, acc_ref, bias_ref, s_ref):
    step = pl.program_id(0)
    kb = kb_tab[step]
    nkb = nkb_tab[step]
    n_sub = KEY_BLOCK // KEY_SUB

    @pl.when(kb == 0)
    def _():
        m_ref[...] = jnp.full(m_ref.shape, NEG_BIG, F32)
        acc_ref[...] = jnp.zeros_like(acc_ref)

    for kc in range(n_sub):
        ks = slice(kc * KEY_SUB, (kc + 1) * KEY_SUB)
        bias_ref[:, ks] = (mask_ref[:, ks].astype(F32) - 1.0) * (-NEG_BIG)

    ones = jnp.ones((KEY_SUB, LANES), BF16)
    q_refs = (qa_ref, qb_ref)
    lanes_of = lambda h: slice((h // 2) * LANES, (h // 2 + 1) * LANES)
    m_old, m_new = [], []
    for h in range(N_HEADS):
        sl = lanes_of(h)
        q = q_refs[h % 2][:, sl]
        bm = None
        for kc in range(n_sub):
            ks = slice(kc * KEY_SUB, (kc + 1) * KEY_SUB)
            s = _dot_nt(q, k_ref[ks, sl]) + bias_ref[:, ks]
            s_ref[h, :, ks] = s
            mx = jnp.max(s, axis=1, keepdims=True)
            bm = mx if bm is None else jnp.maximum(bm, mx)
        m_old.append(m_ref[h])
        m_new.append(jnp.maximum(m_old[h], bm))
    for h in range(N_HEADS):
        sl = lanes_of(h)
        m2 = jnp.concatenate([m_new[h]] * (KEY_SUB // LANES), axis=1)
        pv = None
        for kc in range(n_sub):
            ks = slice(kc * KEY_SUB, (kc + 1) * KEY_SUB)
            prob = jnp.exp2(s_ref[h, :, ks] - m2).astype(BF16)
            v_aug = jnp.concatenate([v_ref[ks, sl], ones], axis=1)
            d = jnp.dot(prob, v_aug, preferred_element_type=F32)
            pv = d if pv is None else pv + d
        alpha = jnp.exp2(m_old[h] - m_new[h])
        acc_ref[h] = acc_ref[h] * jnp.concatenate([alpha, alpha], axis=1) + pv
        m_ref[h] = m_new[h]

    @pl.when(kb == nkb - 1)
    def _():
        lane_h0 = _iota((Q_BLOCK, LANES), 1) < HEAD_DIM
        for p in range(N_PAIRS):
            sl = slice(p * LANES, (p + 1) * LANES)
            a0 = acc_ref[2 * p]
            a1 = acc_ref[2 * p + 1]
            o_ref[:, sl] = jnp.where(lane_h0, a0[:, 0:LANES] / a0[:, LANES:2 * LANES],
                                     a1[:, 0:LANES] / a1[:, LANES:2 * LANES])


def _attn(qa, qb, k, v, mask):
    s = k.shape[0]
    qb_t, kb_t, nkb_t = _block_tables(s)
    qspec = pl.BlockSpec((Q_BLOCK, D_BRANCH), lambda i, qb, kb, nk: (qb[i], 0))
    kspec = pl.BlockSpec((KEY_BLOCK, D_BRANCH), lambda i, qb, kb, nk: (kb[i], 0))
    grid_spec = pltpu.PrefetchScalarGridSpec(
        num_scalar_prefetch=3,
        grid=(len(qb_t),),
        in_specs=[qspec, qspec, kspec, kspec,
                  pl.BlockSpec((Q_BLOCK, KEY_BLOCK), lambda i, qb, kb, nk: (qb[i], kb[i]))],
        out_specs=qspec,
        scratch_shapes=[pltpu.VMEM((N_HEADS, Q_BLOCK, LANES), F32),
                        pltpu.VMEM((N_HEADS, Q_BLOCK, 2 * LANES), F32),
                        pltpu.VMEM((Q_BLOCK, KEY_BLOCK), F32),
                        pltpu.VMEM((N_HEADS, Q_BLOCK, KEY_BLOCK), F32)],
    )
    return pl.pallas_call(
        _attn_kernel,
        grid_spec=grid_spec,
        out_shape=jax.ShapeDtypeStruct((s, D_BRANCH), F32),
        compiler_params=pltpu.CompilerParams(dimension_semantics=("arbitrary",),
                                             vmem_limit_bytes=VMEM_LIMIT),
        name="attn",
    )(jnp.asarray(qb_t), jnp.asarray(kb_t), jnp.asarray(nkb_t), qa, qb, k, v, mask)


def _outproj_kernel(x_ref, za_ref, yb_ref, gb_ref, pg_ref, gate_ref, wa_ref, wb_ref, wo_ref,
                    o_ref):
    d = x_ref.shape[1]
    gb = gb_ref[...]
    zb = yb_ref[...] * (gb * _sigmoid(gb))
    ya = jnp.dot(za_ref[...].astype(BF16), wa_ref[...], preferred_element_type=F32)
    yb = jnp.dot(zb.astype(BF16), wb_ref[...], preferred_element_type=F32)
    merged = _sigmoid(pg_ref[:, 0:d]) * ya + _sigmoid(pg_ref[:, d:2 * d]) * yb
    out = jnp.dot(merged.astype(BF16), wo_ref[...], preferred_element_type=F32)
    o_ref[...] = x_ref[...] + gate_ref[...] * out


def _outproj(x2, za, yb, pq, pg, gate, wa, wb, wo, tm):
    s, d = x2.shape
    row = lambda n: pl.BlockSpec((tm, n), lambda i: (i, 0))
    wsp = lambda w: pl.BlockSpec(w.shape, lambda i: (0, 0))
    gb_col = 3 * D_BRANCH // D_BRANCH
    return pl.pallas_call(
        _outproj_kernel,
        grid=(s // tm,),
        in_specs=[row(d), row(D_BRANCH), row(D_BRANCH),
                  pl.BlockSpec((tm, D_BRANCH), lambda i: (i, gb_col)),
                  row(2 * d), pl.BlockSpec((1, d), lambda i: (0, 0)),
                  wsp(wa), wsp(wb), wsp(wo)],
        out_specs=row(d),
        out_shape=jax.ShapeDtypeStruct((s, d), F32),
        compiler_params=pltpu.CompilerParams(dimension_semantics=("arbitrary",),
                                             vmem_limit_bytes=VMEM_LIMIT),
        name="outproj",
    )(x2, za, yb, pq, pg, gate, wa, wb, wo)


def _pad_cols(w, n):
    return jnp.pad(w, ((0, 0), (0, n - w.shape[1])))


def _layer(x2, c, norm_w, w_ada, b_ada, w_in, mu, w0, w_up, a0, a_up, k_k, k_a, r_k, gn_w,
           gn_b, q_gain, k_gain, w_a_out, w_b_out, w_o):
    s, d = x2.shape
    db = D_BRANCH
    assert s % KEY_BLOCK == 0 and d % LANES == 0
    topk = min(TOPK_MAX, s // 4)

    mod = _ada(jnp.broadcast_to(c, (8, d)), w_ada, b_ada[None, :])[0:1]
    shift, scale, gate = mod[:, 0:d], mod[:, d:2 * d], mod[:, 2 * d:3 * d]

    n_rwkv = 4 * db + 2 * LORA
    def rwkv_cols(w):
        return jnp.concatenate(
            [w[:, 0:3 * db], w[:, 3 * db + 2 * LORA:n_rwkv],
             _pad_cols(w[:, 3 * db:3 * db + LORA], LANES),
             _pad_cols(w[:, 3 * db + LORA:3 * db + 2 * LORA], LANES)], axis=1)
    wa = rwkv_cols(w_in[:, 0:n_rwkv]).astype(BF16)
    mu_a = rwkv_cols(mu[None, :])
    o = n_rwkv
    wq = w_in[:, o:o + 4 * db].astype(BF16)
    o += 4 * db
    w_iq = w_in[:, o:o + N_HEADS * HEAD_DIM].reshape(d, N_HEADS, 1, HEAD_DIM)
    w_iq = jnp.broadcast_to(w_iq, (d, N_HEADS, 4, HEAD_DIM)).reshape(d, N_HEADS * 4 * HEAD_DIM)
    o += N_HEADS * HEAD_DIM
    w_ik = jnp.tile(w_in[:, o:o + HEAD_DIM], (1, 4))
    o += HEAD_DIM
    w_iw = _pad_cols(w_in[:, o:o + N_HEADS], LANES)
    o += N_HEADS
    wi = jnp.concatenate([w_iq, w_ik, w_iw], axis=1).astype(BF16)
    wg = w_in[:, o:o + 2 * d].astype(BF16)

    pa, pq, pi, pg = _inproj(x2, norm_w[None, :], scale, shift, wa, wq, wi, wg, tm=256)

    pad_rows = lambda w: jnp.pad(w, ((0, LANES - w.shape[0]), (0, 0)))
    za = _rwkv(pa, mu_a, w0[None, :], a0[None, :], k_k[None, :], k_a[None, :],
               r_k.reshape(1, db), gn_w[None, :], gn_b[None, :], pad_rows(w_up), pad_rows(a_up),
               tm=512)

    qg = jnp.tile(q_gain, N_HEADS)[None, :]
    kg = jnp.tile(k_gain, N_HEADS)[None, :]
    qa, qb, kn, vb, iq, ik, iw = _dprep(pq, pi, qg, kg, tm=512)
    mask = _select(iq, iw, ik, topk)
    yb = _attn(qa, qb, kn, vb, mask)

    return _outproj(x2, za, yb, pq, pg, gate, w_a_out.astype(BF16), w_b_out.astype(BF16),
                    w_o.astype(BF16), tm=512)


def kernel(x, c, norm_w, w_ada, b_ada, w_in, mu, w0, w_up, a0, a_up, k_k, k_a, r_k, gn_w, gn_b,
           q_gain, k_gain, w_a_out, w_b_out, w_o):
    b, s, d = x.shape
    outs = []
    for bi in range(b):
        xb = x[bi]
        for l in range(norm_w.shape[0]):
            xb = _layer(xb, c[bi:bi + 1], norm_w[l], w_ada[l], b_ada[l], w_in[l], mu[l], w0[l],
                        w_up[l], a0[l], a_up[l], k_k[l], k_a[l], r_k[l], gn_w[l], gn_b[l],
                        q_gain[l], k_gain[l], w_a_out[l], w_b_out[l], w_o[l])
        outs.append(xb)
    return jnp.stack(outs, axis=0)
```
